```python
import math
import jax, jax.numpy as jnp
from jax import lax
import numpy as np

D_MODEL = 1024
BATCH = 16
SEQ = 4096
DEPTH = 4

N_META = 16
CHUNK = 128
EPS = 1e-6
NEG = -1e30

SSD_D_INNER = D_MODEL
SSD_HEAD_DIM = 64
SSD_HEADS = SSD_D_INNER // SSD_HEAD_DIM
SSD_GROUPS = 2
SSD_STATE = 128
SSD_CONV = 4
SSD_CONV_DIM = SSD_D_INNER + 2 * SSD_GROUPS * SSD_STATE

FOX_HEADS = 8
FOX_HEAD_DIM = 128
FOX_WIDTH = FOX_HEADS * FOX_HEAD_DIM

S5_WIDTH = D_MODEL
S5_GROUP = 16
S5_GROUPS = S5_WIDTH // S5_GROUP
S5_STATE = 64

N_BRANCH = 3
D_FF = ((8 * D_MODEL + 3 * 256 - 1) // (3 * 256)) * 256

IN_SPLITS = (SSD_D_INNER, SSD_CONV_DIM, SSD_HEADS, 3 * FOX_WIDTH, FOX_HEADS, S5_WIDTH, N_BRANCH * D_MODEL)
D_IN = SSD_D_INNER + SSD_CONV_DIM + SSD_HEADS + 3 * FOX_WIDTH + FOX_HEADS + S5_WIDTH + N_BRANCH * D_MODEL

kernel_name = 'hybrid_ssd_fox_s5_gated_block'


def rmsnorm(x, w):
    xf = x.astype(jnp.float32)
    y = xf * lax.rsqrt(jnp.mean(xf * xf, axis=-1, keepdims=True) + EPS)
    return (y * w.astype(jnp.float32)).astype(x.dtype)


def causal_dwconv(x, w, b):
    c = x.shape[-1]
    y = lax.conv_general_dilated(x, w[:, None, :].astype(x.dtype), window_strides=(1,),
                                 padding=[(w.shape[0] - 1, 0)],
                                 dimension_numbers=('NWC', 'WIO', 'NWC'),
                                 feature_group_count=c)
    return y + b.astype(x.dtype)


def segsum(a):
    t = a.shape[-1]
    ar = jnp.broadcast_to(a[..., :, None], a.shape + (t,))
    cs = jnp.cumsum(jnp.where(jnp.tril(jnp.ones((t, t), bool), -1), ar, 0.0), axis=-2)
    return jnp.where(jnp.tril(jnp.ones((t, t), bool)), cs, -jnp.inf)


def pad_front(t, pad, axis=1):
    widths = [(0, 0)] * t.ndim
    widths[axis] = (pad, 0)
    return jnp.pad(t, widths)


def ssd_mixer(z, xbc, dt_raw, conv_w, conv_b, dt_bias, a_log, d_skip, norm_w):
    f32 = jnp.float32
    b, L, _ = xbc.shape
    lp = ((L + CHUNK - 1) // CHUNK) * CHUNK
    pad = lp - L
    nc = lp // CHUNK
    G, R, P, N = SSD_GROUPS, SSD_HEADS // SSD_GROUPS, SSD_HEAD_DIM, SSD_STATE
    xbc = jax.nn.silu(causal_dwconv(xbc, conv_w, conv_b))
    xs, bm, cm = jnp.split(xbc, [SSD_D_INNER, SSD_D_INNER + G * N], axis=-1)
    xs = xs.astype(f32)
    dt = jax.nn.softplus(dt_raw.astype(f32) + dt_bias.astype(f32))
    a = -jnp.exp(a_log.astype(f32))
    xdt = (xs.reshape(b, L, G, R, P) * dt.reshape(b, L, G, R)[..., None])
    adt = (dt * a).reshape(b, L, G, R)
    xc = pad_front(xdt, pad).reshape(b, nc, CHUNK, G, R, P)
    bc = pad_front(bm.astype(f32).reshape(b, L, G, N), pad).reshape(b, nc, CHUNK, G, N)
    cc = pad_front(cm.astype(f32).reshape(b, L, G, N), pad).reshape(b, nc, CHUNK, G, N)
    ac = pad_front(adt, pad).reshape(b, nc, CHUNK, G, R).transpose(0, 3, 4, 1, 2)
    a_cs = jnp.cumsum(ac, axis=-1)
    lmat = jnp.exp(segsum(ac))
    y_diag = jnp.einsum('bclgn,bcsgn,bgrcls,bcsgrp->bclgrp', cc, bc, lmat, xc)
    decay_states = jnp.exp(a_cs[..., -1:] - a_cs)
    states = jnp.einsum('bclgn,bgrcl,bclgrp->bcgrpn', bc, decay_states, xc)
    states = jnp.concatenate([jnp.zeros_like(states[:, :1]), states], axis=1)
    chunk_decay = jnp.exp(segsum(pad_front(a_cs[..., -1], 1, axis=3)))
    states = jnp.einsum('bgrzc,bcgrpn->bzgrpn', chunk_decay, states)[:, :-1]
    y_off = jnp.einsum('bclgn,bcgrpn,bgrcl->bclgrp', cc, states, jnp.exp(a_cs))
    y = (y_diag + y_off).reshape(b, lp, SSD_D_INNER)[:, pad:]
    y = y + xs * jnp.repeat(d_skip.astype(f32), SSD_HEAD_DIM)
    y = y * jax.nn.silu(z.astype(f32))
    return rmsnorm(y, norm_w).astype(z.dtype)


def fox_mixer(qkv, f_raw, b_f):
    f32 = jnp.float32
    b, L, _ = qkv.shape
    lp = ((L + CHUNK - 1) // CHUNK) * CHUNK
    pad = lp - L
    nblk = lp // CHUNK
    q, k, v = jnp.split(qkv, 3, axis=-1)
    q, k, v = [pad_front(t.reshape(b, L, FOX_HEADS, FOX_HEAD_DIM), pad) for t in (q, k, v)]
    logf = jax.nn.log_sigmoid(f_raw.astype(f32) + b_f.astype(f32))
    cum = pad_front(jnp.cumsum(logf, axis=1).transpose(0, 2, 1), pad, axis=2)
    key_pos = jnp.arange(lp)
    key_valid = key_pos >= pad
    scale = FOX_HEAD_DIM ** -0.5

    def block(i):
        start = i * CHUNK
        qb = lax.dynamic_slice_in_dim(q, start, CHUNK, axis=1)
        cq = lax.dynamic_slice_in_dim(cum, start, CHUNK, axis=2)
        s = jnp.einsum('bqhd,bkhd->bhqk', qb, k).astype(f32) * scale
        s = s + cq[..., :, None] - cum[:, :, None, :]
        qpos = start + jnp.arange(CHUNK)
        mask = (key_pos[None, :] <= qpos[:, None]) & key_valid[None, :]
        p = jax.nn.softmax(jnp.where(mask, s, NEG), axis=-1)
        return jnp.einsum('bhqk,bkhd->bqhd', p.astype(v.dtype), v)

    out = lax.map(block, jnp.arange(nblk))
    return out.transpose(1, 0, 2, 3, 4).reshape(b, lp, FOX_WIDTH)[:, pad:]


def s5_mixer(u, lam_re, lam_im, b_re, b_im, c_re, c_im, log_step, d_skip, w_glu):
    f32 = jnp.float32
    b, L, _ = u.shape
    uf = u.astype(f32)
    lam = lax.complex(lam_re.astype(f32), lam_im.astype(f32))
    step = jnp.exp(log_step.astype(f32))[:, None]
    lam_bar = jnp.exp(lam * step)
    b_bar = ((lam_bar - 1.0) / lam)[..., None] * lax.complex(b_re.astype(f32), b_im.astype(f32))
    cmat = lax.complex(c_re.astype(f32), c_im.astype(f32))
    bu = jnp.einsum('gpc,blgc->blgp', b_bar, uf.reshape(b, L, S5_GROUPS, S5_GROUP))
    a_seq = jnp.broadcast_to(lam_bar, (1, L) + lam_bar.shape)

    def combine(e1, e2):
        a1, h1 = e1
        a2, h2 = e2
        return a1 * a2, a2 * h1 + h2

    _, h = lax.associative_scan(combine, (a_seq, bu), axis=1)
    y = jnp.real(jnp.einsum('gcp,blgp->blgc', cmat, h)).reshape(b, L, S5_WIDTH)
    y = jax.nn.gelu(y + d_skip.astype(f32) * uf)
    y = y * jax.nn.sigmoid(y @ w_glu.astype(f32))
    return y.astype(u.dtype)


def setup_inputs(seed: int = 0) -> dict:
    key = jax.random.key(seed)
    ks = jax.random.split(key, 32)
    f32 = jnp.float32

    def nrm(k, shape, scale):
        return jax.random.normal(k, shape, f32) * scale

    G, P = S5_GROUPS, S5_STATE
    dt0 = jnp.exp(jax.random.uniform(ks[5], (DEPTH, SSD_HEADS), f32, minval=math.log(1e-3), maxval=math.log(1e-1)))
    n_idx = jnp.arange(P, dtype=f32)
    return {
        'x': nrm(ks[0], (BATCH, SEQ, D_MODEL), 1.0),
        'meta': nrm(ks[1], (N_META, D_MODEL), 1.0),
        'norm1': 1.0 + nrm(ks[2], (DEPTH, D_MODEL), 0.01),
        'w_in': nrm(ks[3], (DEPTH, D_MODEL, D_IN), D_MODEL ** -0.5),
        'ssd_conv_w': nrm(ks[4], (DEPTH, SSD_CONV, SSD_CONV_DIM), SSD_CONV ** -0.5),
        'ssd_conv_b': nrm(ks[6], (DEPTH, SSD_CONV_DIM), 0.01),
        'ssd_dt_bias': dt0 + jnp.log(-jnp.expm1(-dt0)),
        'ssd_a_log': jnp.log(jax.random.uniform(ks[7], (DEPTH, SSD_HEADS), f32, minval=1.0, maxval=16.0)),
        'ssd_d': 1.0 + nrm(ks[8], (DEPTH, SSD_HEADS), 0.1),
        'ssd_norm': 1.0 + nrm(ks[9], (DEPTH, SSD_D_INNER), 0.01),
        'fox_bf': jax.random.uniform(ks[10], (DEPTH, FOX_HEADS), f32, minval=1.0, maxval=4.0),
        's5_lam_re': -0.5 + nrm(ks[11], (DEPTH, G, P), 0.01),
        's5_lam_im': math.pi * n_idx + nrm(ks[12], (DEPTH, G, P), 0.01),
        's5_b_re': nrm(ks[13], (DEPTH, G, P, S5_GROUP), (2 * S5_GROUP) ** -0.5),
        's5_b_im': nrm(ks[14], (DEPTH, G, P, S5_GROUP), (2 * S5_GROUP) ** -0.5),
        's5_c_re': nrm(ks[15], (DEPTH, G, S5_GROUP, P), P ** -0.5),
        's5_c_im': nrm(ks[16], (DEPTH, G, S5_GROUP, P), P ** -0.5),
        's5_log_step': jax.random.uniform(ks[17], (DEPTH, G), f32, minval=math.log(1e-3), maxval=math.log(1e-1)),
        's5_d': nrm(ks[18], (DEPTH, S5_WIDTH), 1.0),
        's5_w_glu': nrm(ks[19], (DEPTH, S5_WIDTH, S5_WIDTH), S5_WIDTH ** -0.5),
        'w_branch': nrm(ks[20], (DEPTH, N_BRANCH, D_MODEL, D_MODEL), D_MODEL ** -0.5),
        'w_out': nrm(ks[21], (DEPTH, D_MODEL, D_MODEL), D_MODEL ** -0.5),
        'norm2': 1.0 + nrm(ks[22], (DEPTH, D_MODEL), 0.01),
        'w_ffn_in': nrm(ks[23], (DEPTH, D_MODEL, 2 * D_FF), D_MODEL ** -0.5),
        'w_ffn_out': nrm(ks[24], (DEPTH, D_FF, D_MODEL), D_FF ** -0.5),
        'norm_f': 1.0 + nrm(ks[25], (D_MODEL,), 0.01),
    }


def reference(x, meta, norm1, w_in, ssd_conv_w, ssd_conv_b, ssd_dt_bias, ssd_a_log, ssd_d, ssd_norm,
              fox_bf, s5_lam_re, s5_lam_im, s5_b_re, s5_b_im, s5_c_re, s5_c_im, s5_log_step, s5_d,
              s5_w_glu, w_branch, w_out, norm2, w_ffn_in, w_ffn_out, norm_f):
    b = x.shape[0]
    x = jnp.concatenate([jnp.broadcast_to(meta[None].astype(x.dtype), (b, N_META, D_MODEL)), x], axis=1)
    L = x.shape[1]
    offsets = np.cumsum(IN_SPLITS)[:-1].tolist()
    for i in range(DEPTH):
        xn = rmsnorm(x, norm1[i])
        proj = xn @ w_in[i]
        z, xbc, dt_raw, qkv, f_raw, u, gate_logits = jnp.split(proj, offsets, axis=-1)
        y_a = ssd_mixer(z, xbc, dt_raw, ssd_conv_w[i], ssd_conv_b[i], ssd_dt_bias[i], ssd_a_log[i], ssd_d[i], ssd_norm[i])
        y_b = fox_mixer(qkv, f_raw, fox_bf[i])
        y_c = s5_mixer(u, s5_lam_re[i], s5_lam_im[i], s5_b_re[i], s5_b_im[i], s5_c_re[i], s5_c_im[i],
                       s5_log_step[i], s5_d[i], s5_w_glu[i])
        ys = jnp.stack([y_a, y_b, y_c], axis=2)
        branches = jnp.einsum('blnw,nwd->blnd', ys, w_branch[i])
        gates = jax.nn.sigmoid(gate_logits.reshape(b, L, N_BRANCH, D_MODEL))
        x = x + jnp.sum(gates * branches, axis=2) @ w_out[i]
        g, up = jnp.split(rmsnorm(x, norm2[i]) @ w_ffn_in[i], 2, axis=-1)
        x = x + (jax.nn.silu(g) * up) @ w_ffn_out[i]
    return rmsnorm(x, norm_f)[:, N_META:]
```

```python
import functools
import math

import jax
import jax.numpy as jnp
from jax import lax
from jax.experimental import pallas as pl
from jax.experimental.pallas import tpu as pltpu

F32 = jnp.float32
BF16 = jnp.bfloat16

D_MODEL = 1024
N_META = 16
CHUNK = 128
EPS = 1e-6
NEG = -1e30

SSD_HEADS = 16
SSD_HEAD_DIM = 64
SSD_GROUPS = 2
SSD_STATE = 128
SSD_CONV = 4
SSD_CONV_DIM = D_MODEL + 2 * SSD_GROUPS * SSD_STATE

FOX_HEADS = 8
FOX_HEAD_DIM = 128

S5_GROUP = 16
S5_GROUPS = D_MODEL // S5_GROUP
S5_STATE = 64
S5_BLOCKS = 8
S5_BLOCK_GROUPS = S5_GROUPS // S5_BLOCKS
S5_HALF = S5_BLOCK_GROUPS * S5_STATE

D_FF = 2816
SMALL_W = 128
F_LANE0 = SSD_HEADS

SEC_WIDTHS = (D_MODEL, SSD_CONV_DIM, 3 * D_MODEL, D_MODEL, 3 * D_MODEL, SMALL_W)
D_IN_P = sum(SEC_WIDTHS)

ROW_TILE = 512
COL_CHUNK = 512
ATT_TILE = 384
S5_STEPS = 16
VMEM_LIMIT = 56 * 1024 * 1024


def _cparams(sems):
    return pltpu.CompilerParams(dimension_semantics=sems, vmem_limit_bytes=VMEM_LIMIT)


def _resident(shape):
    nd = len(shape)
    return pl.BlockSpec(shape, lambda *_: (0,) * nd, pipeline_mode=pl.Buffered(1))


def _sigmoid(x):
    return 1.0 / (1.0 + jnp.exp(-x))


def _softplus(x):
    return jnp.maximum(x, 0.0) + jnp.log(1.0 + jnp.exp(-jnp.abs(x)))


def _rms(x, w):
    ms = jnp.mean(x * x, axis=-1, keepdims=True)
    return (x * lax.rsqrt(ms + EPS)) * w


def _inproj_kernel(x_ref, nw_ref, w_ref, *out_refs):
    xn = _rms(x_ref[...], nw_ref[...]).astype(BF16)
    off = 0
    for ref in out_refs:
        width = ref.shape[1]
        for c0 in range(0, width, COL_CHUNK):
            cw = min(COL_CHUNK, width - c0)
            ref[:, c0:c0 + cw] = jnp.dot(
                xn, w_ref[:, off + c0:off + c0 + cw], preferred_element_type=F32
            ).astype(ref.dtype)
        off += width


def _inproj(x, nw, w):
    rows = x.shape[0]
    dtypes = (BF16, BF16, BF16, BF16, BF16, F32)
    return pl.pallas_call(
        _inproj_kernel,
        grid=(rows // ROW_TILE,),
        in_specs=[
            pl.BlockSpec((ROW_TILE, D_MODEL), lambda i: (i, 0)),
            _resident((1, D_MODEL)),
            _resident((D_MODEL, D_IN_P)),
        ],
        out_specs=[pl.BlockSpec((ROW_TILE, wd), lambda i: (i, 0)) for wd in SEC_WIDTHS],
        out_shape=[jax.ShapeDtypeStruct((rows, wd), dt) for wd, dt in zip(SEC_WIDTHS, dtypes)],
        compiler_params=_cparams(("parallel",)),
        name="inproj",
    )(x, nw, w)


def _ssd_kernel(z_ref, xbc_ref, sm_ref, cw_ref, cb_ref, dtb_ref, alog_ref, dsk_ref, nw_ref,
                o_ref, tail_sc, state_sc, y_sc):
    c = pl.program_id(1)

    @pl.when(c == 0)
    def _():
        tail_sc[...] = jnp.zeros_like(tail_sc)
        state_sc[...] = jnp.zeros_like(state_sc)

    cur = xbc_ref[...].astype(F32)
    full = jnp.concatenate([tail_sc[...], cur], axis=0)
    tail_sc[...] = cur[CHUNK - 8:CHUNK]
    conv = cb_ref[...]
    for k in range(SSD_CONV):
        conv = conv + cw_ref[k:k + 1, :] * full[5 + k:5 + k + CHUNK]
    xc = conv * _sigmoid(conv)
    xs = xc[:, :D_MODEL]
    xs_bf = xs.astype(BF16)
    n_bc = SSD_GROUPS * SSD_STATE
    bm = xc[:, D_MODEL:D_MODEL + n_bc]
    cm = xc[:, D_MODEL + n_bc:]

    dt = _softplus(sm_ref[...] + dtb_ref[...])
    adt = dt * (-jnp.exp(alog_ref[...]))
    row = lax.broadcasted_iota(jnp.int32, (CHUNK, CHUNK), 0)
    col = lax.broadcasted_iota(jnp.int32, (CHUNK, CHUNK), 1)
    causal = row >= col
    tri = jnp.where(causal, 1.0, 0.0).astype(F32)
    acs = jnp.dot(tri, adt, preferred_element_type=F32, precision=lax.Precision.HIGHEST)
    acs_t = acs.T
    dt_t = dt.T

    for g in range(SSD_GROUPS):
        bg = bm[:, g * SSD_STATE:(g + 1) * SSD_STATE]
        cg = cm[:, g * SSD_STATE:(g + 1) * SSD_STATE]
        cb = lax.dot_general(cg.astype(BF16), bg.astype(BF16), (((1,), (1,)), ((), ())),
                             preferred_element_type=F32)
        for hh in range(SSD_HEADS // SSD_GROUPS):
            h = g * (SSD_HEADS // SSD_GROUPS) + hh
            lanes = slice(h * SSD_HEAD_DIM, (h + 1) * SSD_HEAD_DIM)
            acol = acs[:, h:h + 1]
            arow = acs_t[h:h + 1, :]
            dtrow = dt_t[h:h + 1, :]
            alast = acs[CHUNK - 1:CHUNK, h:h + 1]
            lmat = jnp.where(causal, jnp.exp(acol - arow), 0.0)
            gmat = (cb * lmat * dtrow).astype(BF16)
            xh = xs_bf[:, lanes]
            st = state_sc[:, lanes]
            y_h = jnp.dot(gmat, xh, preferred_element_type=F32)
            y_h = y_h + jnp.dot((cg * jnp.exp(acol)).astype(BF16), st.astype(BF16),
                                preferred_element_type=F32)
            wcol = jnp.exp(alast - acol) * dt[:, h:h + 1]
            bw_t = (bg * wcol).T.astype(BF16)
            s_h = jnp.dot(bw_t, xh, preferred_element_type=F32)
            state_sc[:, lanes] = st * jnp.exp(alast) + s_h
            y_sc[:, lanes] = y_h

    y = y_sc[...] + xs * dsk_ref[...]
    zf = z_ref[...].astype(F32)
    y = y * (zf * _sigmoid(zf))
    o_ref[...] = _rms(y, nw_ref[...]).astype(o_ref.dtype)


def _ssd(z, xbc, small, cw, cb, dtb, alog, dsk, nw, *, batch):
    lp = z.shape[0]
    seq_blk = lambda w: pl.BlockSpec((CHUNK, w), lambda b, c: (c, b))
    return pl.pallas_call(
        _ssd_kernel,
        grid=(batch, lp // CHUNK),
        in_specs=[
            seq_blk(D_MODEL), seq_blk(SSD_CONV_DIM), seq_blk(SMALL_W),
            _resident((SSD_CONV, SSD_CONV_DIM)), _resident((1, SSD_CONV_DIM)),
            _resident((1, SMALL_W)), _resident((1, SMALL_W)),
            _resident((1, D_MODEL)), _resident((1, D_MODEL)),
        ],
        out_specs=seq_blk(D_MODEL),
        out_shape=jax.ShapeDtypeStruct((lp, batch * D_MODEL), BF16),
        scratch_shapes=[
            pltpu.VMEM((8, SSD_CONV_DIM), F32),
            pltpu.VMEM((SSD_STATE, D_MODEL), F32),
            pltpu.VMEM((CHUNK, D_MODEL), F32),
        ],
        compiler_params=_cparams(("parallel", "arbitrary")),
        name="ssd",
    )(z, xbc, small, cw, cb, dtb, alog, dsk, nw)


def _cum_kernel(sm_ref, bf_ref, cum_ref, cum_t_ref, carry_sc):
    c = pl.program_id(1)

    @pl.when(c == 0)
    def _():
        carry_sc[...] = jnp.zeros_like(carry_sc)

    x = sm_ref[...] + bf_ref[...]
    logf = jnp.minimum(x, 0.0) - jnp.log(1.0 + jnp.exp(-jnp.abs(x)))
    row = lax.broadcasted_iota(jnp.int32, (CHUNK, CHUNK), 0)
    col = lax.broadcasted_iota(jnp.int32, (CHUNK, CHUNK), 1)
    tri = jnp.where(row >= col, 1.0, 0.0).astype(F32)
    cs = jnp.dot(tri, logf, preferred_element_type=F32, precision=lax.Precision.HIGHEST)
    cs = cs + carry_sc[...]
    carry_sc[...] = cs[CHUNK - 1:CHUNK, :]
    cum_ref[...] = cs
    cum_t_ref[...] = cs.T


def _fox_cum(small, bf, *, batch):
    lp = small.shape[0]
    return pl.pallas_call(
        _cum_kernel,
        grid=(batch, lp // CHUNK),
        in_specs=[pl.BlockSpec((CHUNK, SMALL_W), lambda b, c: (c, b)), _resident((1, SMALL_W))],
        out_specs=[pl.BlockSpec((CHUNK, SMALL_W), lambda b, c: (c, b)),
                   pl.BlockSpec((SMALL_W, CHUNK), lambda b, c: (b, c))],
        out_shape=[jax.ShapeDtypeStruct((lp, batch * SMALL_W), F32),
                   jax.ShapeDtypeStruct((batch * SMALL_W, lp), F32)],
        scratch_shapes=[pltpu.VMEM((1, SMALL_W), F32)],
        compiler_params=_cparams(("parallel", "arbitrary")),
        name="fox_cum",
    )(small, bf)


def _attn_kernel(q_ref, k_ref, v_ref, cq_ref, ck_ref, o_ref):
    tq = ATT_TILE
    i = pl.program_id(1)
    h = pl.program_id(0) % FOX_HEADS
    scale = FOX_HEAD_DIM ** -0.5
    q = q_ref[...]
    lane = lax.broadcasted_iota(jnp.int32, (tq, SMALL_W), 1)
    cq = jnp.sum(jnp.where(lane == F_LANE0 + h, cq_ref[...], 0.0), axis=1, keepdims=True)
    row = lax.broadcasted_iota(jnp.int32, (tq, tq), 0)
    col = lax.broadcasted_iota(jnp.int32, (tq, tq), 1)

    def block(kj, carry, masked):
        m, l, acc = carry
        ks = pl.multiple_of(kj * tq, tq)
        kb = k_ref[pl.ds(ks, tq), :]
        vb = v_ref[pl.ds(ks, tq), :]
        s = lax.dot_general(q, kb, (((1,), (1,)), ((), ())), preferred_element_type=F32) * scale
        s = (s + cq) - ck_ref[0, :, pl.ds(ks, tq)]
        if masked:
            s = jnp.where(row >= col, s, NEG)
        m_new = jnp.maximum(m, jnp.max(s, axis=1, keepdims=True))
        p = jnp.exp(s - m_new)
        alpha = jnp.exp(m - m_new)
        l = alpha * l + jnp.sum(p, axis=1, keepdims=True)
        acc = alpha * acc + jnp.dot(p.astype(BF16), vb, preferred_element_type=F32)
        return m_new, l, acc

    init = (jnp.full((tq, 1), NEG, F32), jnp.zeros((tq, 1), F32), jnp.zeros((tq, FOX_HEAD_DIM), F32))
    carry = lax.fori_loop(0, i, lambda kj, cr: block(kj, cr, False), init)
    _, l, acc = block(i, carry, True)
    o_ref[...] = (acc / l).astype(o_ref.dtype)


def _attention(qkv, cum, cum_t, *, batch):
    lp = qkv.shape[0]
    hd = FOX_HEAD_DIM
    nh = FOX_HEADS
    cum_t3 = cum_t.reshape(batch * SMALL_W, 1, lp)
    return pl.pallas_call(
        _attn_kernel,
        grid=(batch * nh, lp // ATT_TILE),
        in_specs=[
            pl.BlockSpec((ATT_TILE, hd), lambda bh, i: (i, (bh // nh) * 3 * nh + bh % nh)),
            pl.BlockSpec((lp, hd), lambda bh, i: (0, (bh // nh) * 3 * nh + nh + bh % nh)),
            pl.BlockSpec((lp, hd), lambda bh, i: (0, (bh // nh) * 3 * nh + 2 * nh + bh % nh)),
            pl.BlockSpec((ATT_TILE, SMALL_W), lambda bh, i: (i, bh // nh)),
            pl.BlockSpec((1, 1, lp), lambda bh, i: ((bh // nh) * SMALL_W + F_LANE0 + bh % nh, 0, 0)),
        ],
        out_specs=pl.BlockSpec((ATT_TILE, hd), lambda bh, i: (i, bh)),
        out_shape=jax.ShapeDtypeStruct((lp, batch * nh * hd), BF16),
        compiler_params=_cparams(("parallel", "arbitrary")),
        name="fox_attn",
    )(qkv, qkv, qkv, cum, cum_t3)


def _s5_kernel(u_ref, bblk_ref, are_ref, aim_ref, cblk_ref, dsk_ref, wglu_ref, o_ref,
               h_sc, carry_sc, y_sc, *, batch):
    blk = 2 * S5_HALF

    @pl.when(pl.program_id(0) == 0)
    def _():
        carry_sc[...] = jnp.zeros_like(carry_sc)

    u = u_ref[...]
    for j in range(S5_BLOCKS):
        h_sc[:, j * blk:(j + 1) * blk] = jnp.dot(
            u[:, j * 128:(j + 1) * 128], bblk_ref[j], preferred_element_type=F32)

    for j in range(S5_BLOCKS):
        re_l = slice(j * blk, j * blk + S5_HALF)
        im_l = slice(j * blk + S5_HALF, (j + 1) * blk)
        a_re = are_ref[:, j * S5_HALF:(j + 1) * S5_HALF]
        a_im = aim_ref[:, j * S5_HALF:(j + 1) * S5_HALF]

        def step(t, carry, re_l=re_l, im_l=im_l, a_re=a_re, a_im=a_im):
            h_re, h_im = carry
            r = pl.multiple_of(t * batch, batch)
            n_re = a_re * h_re - a_im * h_im + h_sc[pl.ds(r, batch), re_l]
            n_im = a_re * h_im + a_im * h_re + h_sc[pl.ds(r, batch), im_l]
            h_sc[pl.ds(r, batch), re_l] = n_re
            h_sc[pl.ds(r, batch), im_l] = n_im
            return n_re, n_im

        h_re, h_im = lax.fori_loop(0, S5_STEPS, step, (carry_sc[:, re_l], carry_sc[:, im_l]))
        carry_sc[:, re_l] = h_re
        carry_sc[:, im_l] = h_im

    for j in range(S5_BLOCKS):
        y_sc[:, j * 128:(j + 1) * 128] = jnp.dot(
            h_sc[:, j * blk:(j + 1) * blk].astype(BF16), cblk_ref[j], preferred_element_type=F32)

    y = y_sc[...] + dsk_ref[...] * u.astype(F32)
    y = 0.5 * y * (1.0 + jnp.tanh(0.7978845608028654 * (y + 0.044715 * (y * y * y))))
    gl = jnp.dot(y.astype(BF16), wglu_ref[...], preferred_element_type=F32)
    o_ref[...] = (y * _sigmoid(gl)).astype(o_ref.dtype)


def _s5(u, bblk, a_re, a_im, cblk, dsk, wglu, *, batch):
    rows_total = u.shape[0]
    rows = S5_STEPS * batch
    n_state = S5_BLOCKS * 2 * S5_HALF
    return pl.pallas_call(
        functools.partial(_s5_kernel, batch=batch),
        grid=(rows_total // rows,),
        in_specs=[
            pl.BlockSpec((rows, D_MODEL), lambda i: (i, 0)),
            _resident((S5_BLOCKS, 128, 2 * S5_HALF)),
            _resident((1, S5_BLOCKS * S5_HALF)), _resident((1, S5_BLOCKS * S5_HALF)),
            _resident((S5_BLOCKS, 2 * S5_HALF, 128)),
            _resident((1, D_MODEL)),
            _resident((D_MODEL, D_MODEL)),
        ],
        out_specs=pl.BlockSpec((rows, D_MODEL), lambda i: (i, 0)),
        out_shape=jax.ShapeDtypeStruct((rows_total, D_MODEL), BF16),
        scratch_shapes=[
            pltpu.VMEM((rows, n_state), F32),
            pltpu.VMEM((batch, n_state), F32),
            pltpu.VMEM((rows, D_MODEL), F32),
        ],
        compiler_params=_cparams(("arbitrary",)),
        name="s5",
    )(u, bblk, a_re, a_im, cblk, dsk, wglu)


def _merge_kernel(ya_ref, yb_ref, yc_ref, gate_ref, x_ref, wb_ref, wo_ref, o_ref):
    acc = None
    for n, y_ref in enumerate((ya_ref, yb_ref, yc_ref)):
        br = jnp.dot(y_ref[...], wb_ref[n], preferred_element_type=F32)
        gt = _sigmoid(gate_ref[:, n * D_MODEL:(n + 1) * D_MODEL].astype(F32))
        acc = gt * br if acc is None else acc + gt * br
    o_ref[...] = x_ref[...] + jnp.dot(acc.astype(BF16), wo_ref[...], preferred_element_type=F32)


def _merge(ya, yb, yc, gate, x, wb, wo):
    rows = x.shape[0]
    rblk = lambda w: pl.BlockSpec((ROW_TILE, w), lambda i: (i, 0))
    return pl.pallas_call(
        _merge_kernel,
        grid=(rows // ROW_TILE,),
        in_specs=[rblk(D_MODEL), rblk(D_MODEL), rblk(D_MODEL), rblk(3 * D_MODEL), rblk(D_MODEL),
                  _resident((3, D_MODEL, D_MODEL)), _resident((D_MODEL, D_MODEL))],
        out_specs=rblk(D_MODEL),
        out_shape=jax.ShapeDtypeStruct((rows, D_MODEL), F32),
        compiler_params=_cparams(("parallel",)),
        name="merge",
    )(ya, yb, yc, gate, x, wb, wo)


def _ffn_kernel(x_ref, nw_ref, w1_ref, w2_ref, o_ref):
    x = x_ref[...]
    xn = _rms(x, nw_ref[...]).astype(BF16)
    half = D_FF // 2
    acc = x
    for c0 in (0, half):
        gp = jnp.dot(xn, w1_ref[:, c0:c0 + half], preferred_element_type=F32)
        up = jnp.dot(xn, w1_ref[:, D_FF + c0:D_FF + c0 + half], preferred_element_type=F32)
        act = (gp * _sigmoid(gp) * up).astype(BF16)
        acc = acc + jnp.dot(act, w2_ref[c0:c0 + half, :], preferred_element_type=F32)
    o_ref[...] = acc


def _ffn(x, nw, w1, w2):
    rows = x.shape[0]
    return pl.pallas_call(
        _ffn_kernel,
        grid=(rows // ROW_TILE,),
        in_specs=[pl.BlockSpec((ROW_TILE, D_MODEL), lambda i: (i, 0)), _resident((1, D_MODEL)),
                  _resident((D_MODEL, 2 * D_FF)), _resident((D_FF, D_MODEL))],
        out_specs=pl.BlockSpec((ROW_TILE, D_MODEL), lambda i: (i, 0)),
        out_shape=jax.ShapeDtypeStruct((rows, D_MODEL), F32),
        compiler_params=_cparams(("parallel",)),
        name="ffn",
    )(x, nw, w1, w2)


def _final_norm_kernel(x_ref, nw_ref, o_ref):
    o_ref[...] = _rms(x_ref[...], nw_ref[...])


def _final_norm(x, nw):
    rows = x.shape[0]
    return pl.pallas_call(
        _final_norm_kernel,
        grid=(rows // ROW_TILE,),
        in_specs=[pl.BlockSpec((ROW_TILE, D_MODEL), lambda i: (i, 0)), _resident((1, D_MODEL))],
        out_specs=pl.BlockSpec((ROW_TILE, D_MODEL), lambda i: (i, 0)),
        out_shape=jax.ShapeDtypeStruct((rows, D_MODEL), F32),
        compiler_params=_cparams(("parallel",)),
        name="final_norm",
    )(x, nw)


def _pad_lanes(v, lane0=0):
    depth, n = v.shape
    out = jnp.zeros((depth, 1, SMALL_W), F32)
    return out.at[:, 0, lane0:lane0 + n].set(v.astype(F32))


def _prep_w_in(w_in):
    o = [0, D_MODEL, D_MODEL + SSD_CONV_DIM]
    o.append(o[-1] + SSD_HEADS)
    o.append(o[-1] + 3 * D_MODEL)
    o.append(o[-1] + FOX_HEADS)
    o.append(o[-1] + D_MODEL)
    o.append(o[-1] + 3 * D_MODEL)
    z, xbc, dt, qkv, fr, u, gate = [w_in[..., o[k]:o[k + 1]] for k in range(7)]
    pad = jnp.zeros(w_in.shape[:-1] + (SMALL_W - SSD_HEADS - FOX_HEADS,), w_in.dtype)
    return jnp.concatenate([z, xbc, qkv, u, gate, dt, fr, pad], axis=-1).astype(BF16)


def _prep_s5(lam_re, lam_im, b_re, b_im, c_re, c_im, log_step):
    depth = lam_re.shape[0]
    lam = lax.complex(lam_re.astype(F32), lam_im.astype(F32))
    step = jnp.exp(log_step.astype(F32))[..., None]
    lam_bar = jnp.exp(lam * step)
    b_bar = ((lam_bar - 1.0) / lam)[..., None] * lax.complex(b_re.astype(F32), b_im.astype(F32))
    nb, gb, p, ch = S5_BLOCKS, S5_BLOCK_GROUPS, S5_STATE, S5_GROUP
    eye = jnp.eye(gb, dtype=F32)

    def b_block(part):
        part = part.reshape(depth, nb, gb, p, ch)
        m = jnp.einsum('djgpc,gh->djgchp', part, eye)
        return m.reshape(depth, nb, gb * ch, gb * p)

    bblk = jnp.concatenate([b_block(jnp.real(b_bar)), b_block(jnp.imag(b_bar))], axis=-1)

    def c_block(part):
        part = part.reshape(depth, nb, gb, ch, p)
        m = jnp.einsum('djgcp,gh->djgphc', part, eye)
        return m.reshape(depth, nb, gb * p, gb * ch)

    cblk = jnp.concatenate([c_block(c_re.astype(F32)), c_block(-c_im.astype(F32))], axis=-2)
    a_re = jnp.real(lam_bar).reshape(depth, 1, nb * gb * p)
    a_im = jnp.imag(lam_bar).reshape(depth, 1, nb * gb * p)
    return bblk.astype(BF16), a_re, a_im, cblk.astype(BF16)


def kernel(x, meta, norm1, w_in, ssd_conv_w, ssd_conv_b, ssd_dt_bias, ssd_a_log, ssd_d, ssd_norm,
           fox_bf, s5_lam_re, s5_lam_im, s5_b_re, s5_b_im, s5_c_re, s5_c_im, s5_log_step, s5_d,
           s5_w_glu, w_branch, w_out, norm2, w_ffn_in, w_ffn_out, norm_f):
    batch, seq, _ = x.shape
    length = seq + N_META
    granule = math.lcm(CHUNK, ATT_TILE)
    lp = -(-length // granule) * granule
    assert batch % 8 == 0 and (lp * batch) % ROW_TILE == 0 and lp % S5_STEPS == 0
    rows = lp * batch

    xt = jnp.concatenate([
        jnp.broadcast_to(meta[:, None, :].astype(F32), (N_META, batch, D_MODEL)),
        jnp.transpose(x.astype(F32), (1, 0, 2)),
        jnp.zeros((lp - length, batch, D_MODEL), F32)], axis=0).reshape(rows, D_MODEL)

    bblk, a_re, a_im, cblk = _prep_s5(s5_lam_re, s5_lam_im, s5_b_re, s5_b_im, s5_c_re, s5_c_im,
                                      s5_log_step)
    row3 = lambda v: v.astype(F32)[:, None, :]
    layers = dict(
        norm1=row3(norm1), w_in=_prep_w_in(w_in),
        conv_w=ssd_conv_w.astype(F32), conv_b=row3(ssd_conv_b),
        dt_bias=_pad_lanes(ssd_dt_bias), a_log=_pad_lanes(ssd_a_log),
        ssd_d=row3(jnp.repeat(ssd_d, SSD_HEAD_DIM, axis=-1)), ssd_norm=row3(ssd_norm),
        fox_bf=_pad_lanes(fox_bf, F_LANE0),
        bblk=bblk, a_re=a_re, a_im=a_im, cblk=cblk, s5_d=row3(s5_d), w_glu=s5_w_glu.astype(BF16),
        w_branch=w_branch.astype(BF16), w_out=w_out.astype(BF16), norm2=row3(norm2),
        w_ffn_in=w_ffn_in.astype(BF16), w_ffn_out=w_ffn_out.astype(BF16),
    )

    def layer(xr, p):
        z, xbc, qkv, u, gate, small = _inproj(xr, p['norm1'], p['w_in'])
        seqv = lambda a: a.reshape(lp, batch * a.shape[1])
        small_s = seqv(small)
        y_a = _ssd(seqv(z), seqv(xbc), small_s, p['conv_w'], p['conv_b'], p['dt_bias'], p['a_log'],
                   p['ssd_d'], p['ssd_norm'], batch=batch).reshape(rows, D_MODEL)
        cum, cum_t = _fox_cum(small_s, p['fox_bf'], batch=batch)
        y_b = _attention(seqv(qkv), cum, cum_t, batch=batch).reshape(rows, D_MODEL)
        y_c = _s5(u, p['bblk'], p['a_re'], p['a_im'], p['cblk'], p['s5_d'], p['w_glu'], batch=batch)
        x1 = _merge(y_a, y_b, y_c, gate, xr, p['w_branch'], p['w_out'])
        return _ffn(x1, p['norm2'], p['w_ffn_in'], p['w_ffn_out']), None

    xr, _ = lax.scan(layer, xt, layers)
    out = _final_norm(xr, norm_f.astype(F32)[None, :]).reshape(lp, batch, D_MODEL)
    return jnp.transpose(out[N_META:length], (1, 0, 2))
```

```python
import functools
import math

import jax
import jax.numpy as jnp
from jax import lax
from jax.experimental import pallas as pl
from jax.experimental.pallas import tpu as pltpu

F32 = jnp.float32
BF16 = jnp.bfloat16

D_MODEL = 1024
N_META = 16
CHUNK = 128
EPS = 1e-6
NEG = -1e30

SSD_HEADS = 16
SSD_HEAD_DIM = 64
SSD_GROUPS = 2
SSD_STATE = 128
SSD_CONV = 4
SSD_CONV_DIM = D_MODEL + 2 * SSD_GROUPS * SSD_STATE

FOX_HEADS = 8
FOX_HEAD_DIM = 128

S5_GROUP = 16
S5_GROUPS = D_MODEL // S5_GROUP
S5_STATE = 64
S5_BLOCKS = 8
S5_BLOCK_GROUPS = S5_GROUPS // S5_BLOCKS
S5_HALF = S5_BLOCK_GROUPS * S5_STATE

D_FF = 2816
SMALL_W = 128
F_LANE0 = SSD_HEADS

SEC_WIDTHS = (D_MODEL, SSD_CONV_DIM, 3 * D_MODEL, D_MODEL, 3 * D_MODEL, SMALL_W)
D_IN_P = sum(SEC_WIDTHS)

ROW_TILE = 512
COL_CHUNK = 512
ATT_TILE = 384
S5_STEPS = 16
VMEM_LIMIT = 56 * 1024 * 1024


def _cparams(sems):
    return pltpu.CompilerParams(dimension_semantics=sems, vmem_limit_bytes=VMEM_LIMIT)


def _resident(shape):
    nd = len(shape)
    return pl.BlockSpec(shape, lambda *_: (0,) * nd, pipeline_mode=pl.Buffered(1))


def _sigmoid(x):
    return 1.0 / (1.0 + jnp.exp(-x))


def _softplus(x):
    return jnp.maximum(x, 0.0) + jnp.log(1.0 + jnp.exp(-jnp.abs(x)))


def _rms(x, w):
    ms = jnp.mean(x * x, axis=-1, keepdims=True)
    return (x * lax.rsqrt(ms + EPS)) * w


def _inproj_kernel(x_ref, nw_ref, w_ref, *out_refs):
    xn = _rms(x_ref[...], nw_ref[...]).astype(BF16)
    off = 0
    for ref in out_refs:
        width = ref.shape[1]
        for c0 in range(0, width, COL_CHUNK):
            cw = min(COL_CHUNK, width - c0)
            ref[:, c0:c0 + cw] = jnp.dot(
                xn, w_ref[:, off + c0:off + c0 + cw], preferred_element_type=F32
            ).astype(ref.dtype)
        off += width


def _inproj(x, nw, w):
    rows = x.shape[0]
    dtypes = (BF16, BF16, BF16, BF16, BF16, F32)
    return pl.pallas_call(
        _inproj_kernel,
        grid=(rows // ROW_TILE,),
        in_specs=[
            pl.BlockSpec((ROW_TILE, D_MODEL), lambda i: (i, 0)),
            _resident((1, D_MODEL)),
            _resident((D_MODEL, D_IN_P)),
        ],
        out_specs=[pl.BlockSpec((ROW_TILE, wd), lambda i: (i, 0)) for wd in SEC_WIDTHS],
        out_shape=[jax.ShapeDtypeStruct((rows, wd), dt) for wd, dt in zip(SEC_WIDTHS, dtypes)],
        compiler_params=_cparams(("parallel",)),
        name="inproj",
    )(x, nw, w)


def _ssd_kernel(z_ref, xbc_ref, sm_ref, cw_ref, cb_ref, dtb_ref, alog_ref, dsk_ref, nw_ref,
                o_ref, tail_sc, state_sc, y_sc):
    c = pl.program_id(1)

    @pl.when(c == 0)
    def _():
        tail_sc[...] = jnp.zeros_like(tail_sc)
        state_sc[...] = jnp.zeros_like(state_sc)

    cur = xbc_ref[...].astype(F32)
    full = jnp.concatenate([tail_sc[...], cur], axis=0)
    tail_sc[...] = cur[CHUNK - 8:CHUNK]
    conv = cb_ref[...]
    for k in range(SSD_CONV):
        conv = conv + cw_ref[k:k + 1, :] * full[5 + k:5 + k + CHUNK]
    xc = conv * _sigmoid(conv)
    xs = xc[:, :D_MODEL]
    xs_bf = xs.astype(BF16)
    n_bc = SSD_GROUPS * SSD_STATE
    bm = xc[:, D_MODEL:D_MODEL + n_bc]
    cm = xc[:, D_MODEL + n_bc:]

    dt = _softplus(sm_ref[...] + dtb_ref[...])
    adt = dt * (-jnp.exp(alog_ref[...]))
    row = lax.broadcasted_iota(jnp.int32, (CHUNK, CHUNK), 0)
    col = lax.broadcasted_iota(jnp.int32, (CHUNK, CHUNK), 1)
    causal = row >= col
    tri = jnp.where(causal, 1.0, 0.0).astype(F32)
    acs = jnp.dot(tri, adt, preferred_element_type=F32, precision=lax.Precision.HIGHEST)
    acs_t = acs.T
    dt_t = dt.T

    for g in range(SSD_GROUPS):
        bg = bm[:, g * SSD_STATE:(g + 1) * SSD_STATE]
        cg = cm[:, g * SSD_STATE:(g + 1) * SSD_STATE]
        cb = lax.dot_general(cg.astype(BF16), bg.astype(BF16), (((1,), (1,)), ((), ())),
                             preferred_element_type=F32)
        for hh in range(SSD_HEADS // SSD_GROUPS):
            h = g * (SSD_HEADS // SSD_GROUPS) + hh
            lanes = slice(h * SSD_HEAD_DIM, (h + 1) * SSD_HEAD_DIM)
            acol = acs[:, h:h + 1]
            arow = acs_t[h:h + 1, :]
            dtrow = dt_t[h:h + 1, :]
            alast = acs[CHUNK - 1:CHUNK, h:h + 1]
            lmat = jnp.where(causal, jnp.exp(acol - arow), 0.0)
            gmat = (cb * lmat * dtrow).astype(BF16)
            xh = xs_bf[:, lanes]
            st = state_sc[:, lanes]
            y_h = jnp.dot(gmat, xh, preferred_element_type=F32)
            y_h = y_h + jnp.dot((cg * jnp.exp(acol)).astype(BF16), st.astype(BF16),
                                preferred_element_type=F32)
            wcol = jnp.exp(alast - acol) * dt[:, h:h + 1]
            bw_t = (bg * wcol).T.astype(BF16)
            s_h = jnp.dot(bw_t, xh, preferred_element_type=F32)
            state_sc[:, lanes] = st * jnp.exp(alast) + s_h
            y_sc[:, lanes] = y_h

    y = y_sc[...] + xs * dsk_ref[...]
    zf = z_ref[...].astype(F32)
    y = y * (zf * _sigmoid(zf))
    o_ref[...] = _rms(y, nw_ref[...]).astype(o_ref.dtype)


def _ssd(z, xbc, small, cw, cb, dtb, alog, dsk, nw, *, batch):
    lp = z.shape[1]
    seq_blk = lambda w: pl.BlockSpec((None, CHUNK, w), lambda b, c: (b, c, 0))
    return pl.pallas_call(
        _ssd_kernel,
        grid=(batch, lp // CHUNK),
        in_specs=[
            seq_blk(D_MODEL), seq_blk(SSD_CONV_DIM), seq_blk(SMALL_W),
            _resident((SSD_CONV, SSD_CONV_DIM)), _resident((1, SSD_CONV_DIM)),
            _resident((1, SMALL_W)), _resident((1, SMALL_W)),
            _resident((1, D_MODEL)), _resident((1, D_MODEL)),
        ],
        out_specs=seq_blk(D_MODEL),
        out_shape=jax.ShapeDtypeStruct((batch, lp, D_MODEL), BF16),
        scratch_shapes=[
            pltpu.VMEM((8, SSD_CONV_DIM), F32),
            pltpu.VMEM((SSD_STATE, D_MODEL), F32),
            pltpu.VMEM((CHUNK, D_MODEL), F32),
        ],
        compiler_params=_cparams(("parallel", "arbitrary")),
        name="ssd",
    )(z, xbc, small, cw, cb, dtb, alog, dsk, nw)


def _cum_kernel(sm_ref, bf_ref, cum_ref, cum_t_ref, carry_sc):
    c = pl.program_id(1)

    @pl.when(c == 0)
    def _():
        carry_sc[...] = jnp.zeros_like(carry_sc)

    x = sm_ref[...] + bf_ref[...]
    logf = jnp.minimum(x, 0.0) - jnp.log(1.0 + jnp.exp(-jnp.abs(x)))
    row = lax.broadcasted_iota(jnp.int32, (CHUNK, CHUNK), 0)
    col = lax.broadcasted_iota(jnp.int32, (CHUNK, CHUNK), 1)
    tri = jnp.where(row >= col, 1.0, 0.0).astype(F32)
    cs = jnp.dot(tri, logf, preferred_element_type=F32, precision=lax.Precision.HIGHEST)
    cs = cs + carry_sc[...]
    carry_sc[...] = cs[CHUNK - 1:CHUNK, :]
    cum_ref[...] = cs
    cum_t_ref[...] = cs.T


def _fox_cum(small, bf, *, batch):
    lp = small.shape[1]
    return pl.pallas_call(
        _cum_kernel,
        grid=(batch, lp // CHUNK),
        in_specs=[pl.BlockSpec((None, CHUNK, SMALL_W), lambda b, c: (b, c, 0)),
                  _resident((1, SMALL_W))],
        out_specs=[pl.BlockSpec((None, CHUNK, SMALL_W), lambda b, c: (b, c, 0)),
                   pl.BlockSpec((None, SMALL_W, CHUNK), lambda b, c: (b, 0, c))],
        out_shape=[jax.ShapeDtypeStruct((batch, lp, SMALL_W), F32),
                   jax.ShapeDtypeStruct((batch, SMALL_W, lp), F32)],
        scratch_shapes=[pltpu.VMEM((1, SMALL_W), F32)],
        compiler_params=_cparams(("parallel", "arbitrary")),
        name="fox_cum",
    )(small, bf)


def _attn_kernel(q_ref, k_ref, v_ref, cq_ref, ck_ref, o_ref):
    tq = ATT_TILE
    i = pl.program_id(1)
    h = pl.program_id(0) % FOX_HEADS
    scale = FOX_HEAD_DIM ** -0.5
    q = q_ref[...]
    lane = lax.broadcasted_iota(jnp.int32, (tq, SMALL_W), 1)
    cq = jnp.sum(jnp.where(lane == F_LANE0 + h, cq_ref[...], 0.0), axis=1, keepdims=True)
    row = lax.broadcasted_iota(jnp.int32, (tq, tq), 0)
    col = lax.broadcasted_iota(jnp.int32, (tq, tq), 1)

    def block(kj, carry, masked):
        m, l, acc = carry
        ks = pl.multiple_of(kj * tq, tq)
        kb = k_ref[pl.ds(ks, tq), :]
        vb = v_ref[pl.ds(ks, tq), :]
        s = lax.dot_general(q, kb, (((1,), (1,)), ((), ())), preferred_element_type=F32) * scale
        s = (s + cq) - ck_ref[0, :, pl.ds(ks, tq)]
        if masked:
            s = jnp.where(row >= col, s, NEG)
        m_new = jnp.maximum(m, jnp.max(s, axis=1, keepdims=True))
        p = jnp.exp(s - m_new)
        alpha = jnp.exp(m - m_new)
        l = alpha * l + jnp.sum(p, axis=1, keepdims=True)
        acc = alpha * acc + jnp.dot(p.astype(BF16), vb, preferred_element_type=F32)
        return m_new, l, acc

    init = (jnp.full((tq, 1), NEG, F32), jnp.zeros((tq, 1), F32), jnp.zeros((tq, FOX_HEAD_DIM), F32))
    carry = lax.fori_loop(0, i, lambda kj, cr: block(kj, cr, False), init)
    _, l, acc = block(i, carry, True)
    o_ref[...] = (acc / l).astype(o_ref.dtype)


def _attention(qkv, cum, cum_t, *, batch):
    lp = qkv.shape[1]
    hd = FOX_HEAD_DIM
    nh = FOX_HEADS
    cum_t3 = cum_t.reshape(batch * SMALL_W, 1, lp)
    return pl.pallas_call(
        _attn_kernel,
        grid=(batch * nh, lp // ATT_TILE),
        in_specs=[
            pl.BlockSpec((None, ATT_TILE, hd), lambda bh, i: (bh // nh, i, bh % nh)),
            pl.BlockSpec((None, lp, hd), lambda bh, i: (bh // nh, 0, nh + bh % nh)),
            pl.BlockSpec((None, lp, hd), lambda bh, i: (bh // nh, 0, 2 * nh + bh % nh)),
            pl.BlockSpec((None, ATT_TILE, SMALL_W), lambda bh, i: (bh // nh, i, 0)),
            pl.BlockSpec((1, 1, lp), lambda bh, i: ((bh // nh) * SMALL_W + F_LANE0 + bh % nh, 0, 0)),
        ],
        out_specs=pl.BlockSpec((None, ATT_TILE, hd), lambda bh, i: (bh // nh, i, bh % nh)),
        out_shape=jax.ShapeDtypeStruct((batch, lp, nh * hd), BF16),
        compiler_params=_cparams(("parallel", "arbitrary")),
        name="fox_attn",
    )(qkv, qkv, qkv, cum, cum_t3)


def _s5_kernel(u_ref, bblk_ref, are_ref, aim_ref, cblk_ref, dsk_ref, wglu_ref, o_ref,
               h_sc, carry_sc, y_sc, *, batch):
    @pl.when(pl.program_id(0) == 0)
    def _():
        carry_sc[...] = jnp.zeros_like(carry_sc)

    u = u_ref[...].reshape(batch * S5_STEPS, D_MODEL)
    nt = S5_HALF // 128
    for j in range(S5_BLOCKS):
        bu = jnp.dot(u[:, j * 128:(j + 1) * 128], bblk_ref[j], preferred_element_type=F32)
        for k in range(2 * nt):
            h_sc[j * 2 * nt + k] = bu[:, k * 128:(k + 1) * 128]

    for j in range(S5_BLOCKS):
        k0 = j * 2 * nt
        a_re = [are_ref[:, j * S5_HALF + k * 128:j * S5_HALF + (k + 1) * 128] for k in range(nt)]
        a_im = [aim_ref[:, j * S5_HALF + k * 128:j * S5_HALF + (k + 1) * 128] for k in range(nt)]

        def step(t, carry, k0=k0, a_re=a_re, a_im=a_im):
            at_t = pl.ds(t, batch, stride=S5_STEPS)
            out = []
            for k in range(nt):
                h_re, h_im = carry[k], carry[nt + k]
                out.append(a_re[k] * h_re - a_im[k] * h_im + h_sc[k0 + k, at_t, :])
            for k in range(nt):
                h_re, h_im = carry[k], carry[nt + k]
                out.append(a_re[k] * h_im + a_im[k] * h_re + h_sc[k0 + nt + k, at_t, :])
            for k in range(2 * nt):
                h_sc[k0 + k, at_t, :] = out[k]
            return tuple(out)

        init = tuple(carry_sc[k0 + k] for k in range(2 * nt))
        last = lax.fori_loop(0, S5_STEPS, step, init)
        for k in range(2 * nt):
            carry_sc[k0 + k] = last[k]

    for j in range(S5_BLOCKS):
        hj = jnp.concatenate([h_sc[j * 2 * nt + k] for k in range(2 * nt)], axis=1)
        y_sc[:, j * 128:(j + 1) * 128] = jnp.dot(
            hj.astype(BF16), cblk_ref[j], preferred_element_type=F32)

    y = y_sc[...] + dsk_ref[...] * u.astype(F32)
    y = 0.5 * y * (1.0 + jnp.tanh(0.7978845608028654 * (y + 0.044715 * (y * y * y))))
    gl = jnp.dot(y.astype(BF16), wglu_ref[...], preferred_element_type=F32)
    o_ref[...] = (y * _sigmoid(gl)).astype(o_ref.dtype).reshape(batch, S5_STEPS, D_MODEL)


def _s5(u, bblk, a_re, a_im, cblk, dsk, wglu):
    batch, lp, _ = u.shape
    rows = S5_STEPS * batch
    n_state = S5_BLOCKS * 2 * S5_HALF
    tblk = pl.BlockSpec((batch, S5_STEPS, D_MODEL), lambda i: (0, i, 0))
    return pl.pallas_call(
        functools.partial(_s5_kernel, batch=batch),
        grid=(lp // S5_STEPS,),
        in_specs=[
            tblk,
            _resident((S5_BLOCKS, 128, 2 * S5_HALF)),
            _resident((1, S5_BLOCKS * S5_HALF)), _resident((1, S5_BLOCKS * S5_HALF)),
            _resident((S5_BLOCKS, 2 * S5_HALF, 128)),
            _resident((1, D_MODEL)),
            _resident((D_MODEL, D_MODEL)),
        ],
        out_specs=tblk,
        out_shape=jax.ShapeDtypeStruct((batch, lp, D_MODEL), BF16),
        scratch_shapes=[
            pltpu.VMEM((n_state // 128, rows, 128), F32),
            pltpu.VMEM((n_state // 128, batch, 128), F32),
            pltpu.VMEM((rows, D_MODEL), F32),
        ],
        compiler_params=_cparams(("arbitrary",)),
        name="s5",
    )(u, bblk, a_re, a_im, cblk, dsk, wglu)


def _merge_kernel(ya_ref, yb_ref, yc_ref, gate_ref, x_ref, wb_ref, wo_ref, o_ref):
    acc = None
    for n, y_ref in enumerate((ya_ref, yb_ref, yc_ref)):
        br = jnp.dot(y_ref[...], wb_ref[n], preferred_element_type=F32)
        gt = _sigmoid(gate_ref[:, n * D_MODEL:(n + 1) * D_MODEL].astype(F32))
        acc = gt * br if acc is None else acc + gt * br
    o_ref[...] = x_ref[...] + jnp.dot(acc.astype(BF16), wo_ref[...], preferred_element_type=F32)


def _merge(ya, yb, yc, gate, x, wb, wo):
    rows = x.shape[0]
    rblk = lambda w: pl.BlockSpec((ROW_TILE, w), lambda i: (i, 0))
    return pl.pallas_call(
        _merge_kernel,
        grid=(rows // ROW_TILE,),
        in_specs=[rblk(D_MODEL), rblk(D_MODEL), rblk(D_MODEL), rblk(3 * D_MODEL), rblk(D_MODEL),
                  _resident((3, D_MODEL, D_MODEL)), _resident((D_MODEL, D_MODEL))],
        out_specs=rblk(D_MODEL),
        out_shape=jax.ShapeDtypeStruct((rows, D_MODEL), F32),
        compiler_params=_cparams(("parallel",)),
        name="merge",
    )(ya, yb, yc, gate, x, wb, wo)


def _ffn_kernel(x_ref, nw_ref, w1_ref, w2_ref, o_ref):
    x = x_ref[...]
    xn = _rms(x, nw_ref[...]).astype(BF16)
    half = D_FF // 2
    acc = x
    for c0 in (0, half):
        gp = jnp.dot(xn, w1_ref[:, c0:c0 + half], preferred_element_type=F32)
        up = jnp.dot(xn, w1_ref[:, D_FF + c0:D_FF + c0 + half], preferred_element_type=F32)
        act = (gp * _sigmoid(gp) * up).astype(BF16)
        acc = acc + jnp.dot(act, w2_ref[c0:c0 + half, :], preferred_element_type=F32)
    o_ref[...] = acc


def _ffn(x, nw, w1, w2):
    rows = x.shape[0]
    return pl.pallas_call(
        _ffn_kernel,
        grid=(rows // ROW_TILE,),
        in_specs=[pl.BlockSpec((ROW_TILE, D_MODEL), lambda i: (i, 0)), _resident((1, D_MODEL)),
                  _resident((D_MODEL, 2 * D_FF)), _resident((D_FF, D_MODEL))],
        out_specs=pl.BlockSpec((ROW_TILE, D_MODEL), lambda i: (i, 0)),
        out_shape=jax.ShapeDtypeStruct((rows, D_MODEL), F32),
        compiler_params=_cparams(("parallel",)),
        name="ffn",
    )(x, nw, w1, w2)


def _final_norm_kernel(x_ref, nw_ref, o_ref):
    o_ref[...] = _rms(x_ref[...], nw_ref[...])


def _final_norm(x, nw):
    rows = x.shape[0]
    return pl.pallas_call(
        _final_norm_kernel,
        grid=(rows // ROW_TILE,),
        in_specs=[pl.BlockSpec((ROW_TILE, D_MODEL), lambda i: (i, 0)), _resident((1, D_MODEL))],
        out_specs=pl.BlockSpec((ROW_TILE, D_MODEL), lambda i: (i, 0)),
        out_shape=jax.ShapeDtypeStruct((rows, D_MODEL), F32),
        compiler_params=_cparams(("parallel",)),
        name="final_norm",
    )(x, nw)


def _pad_lanes(v, lane0=0):
    depth, n = v.shape
    out = jnp.zeros((depth, 1, SMALL_W), F32)
    return out.at[:, 0, lane0:lane0 + n].set(v.astype(F32))


def _prep_w_in(w_in):
    o = [0, D_MODEL, D_MODEL + SSD_CONV_DIM]
    o.append(o[-1] + SSD_HEADS)
    o.append(o[-1] + 3 * D_MODEL)
    o.append(o[-1] + FOX_HEADS)
    o.append(o[-1] + D_MODEL)
    o.append(o[-1] + 3 * D_MODEL)
    z, xbc, dt, qkv, fr, u, gate = [w_in[..., o[k]:o[k + 1]] for k in range(7)]
    pad = jnp.zeros(w_in.shape[:-1] + (SMALL_W - SSD_HEADS - FOX_HEADS,), w_in.dtype)
    return jnp.concatenate([z, xbc, qkv, u, gate, dt, fr, pad], axis=-1).astype(BF16)


def _prep_s5(lam_re, lam_im, b_re, b_im, c_re, c_im, log_step):
    depth = lam_re.shape[0]
    lam = lax.complex(lam_re.astype(F32), lam_im.astype(F32))
    step = jnp.exp(log_step.astype(F32))[..., None]
    lam_bar = jnp.exp(lam * step)
    b_bar = ((lam_bar - 1.0) / lam)[..., None] * lax.complex(b_re.astype(F32), b_im.astype(F32))
    nb, gb, p, ch = S5_BLOCKS, S5_BLOCK_GROUPS, S5_STATE, S5_GROUP
    eye = jnp.eye(gb, dtype=F32)

    def b_block(part):
        part = part.reshape(depth, nb, gb, p, ch)
        m = jnp.einsum('djgpc,gh->djgchp', part, eye)
        return m.reshape(depth, nb, gb * ch, gb * p)

    bblk = jnp.concatenate([b_block(jnp.real(b_bar)), b_block(jnp.imag(b_bar))], axis=-1)

    def c_block(part):
        part = part.reshape(depth, nb, gb, ch, p)
        m = jnp.einsum('djgcp,gh->djgphc', part, eye)
        return m.reshape(depth, nb, gb * p, gb * ch)

    cblk = jnp.concatenate([c_block(c_re.astype(F32)), c_block(-c_im.astype(F32))], axis=-2)
    a_re = jnp.real(lam_bar).reshape(depth, 1, nb * gb * p)
    a_im = jnp.imag(lam_bar).reshape(depth, 1, nb * gb * p)
    return bblk.astype(BF16), a_re, a_im, cblk.astype(BF16)


def kernel(x, meta, norm1, w_in, ssd_conv_w, ssd_conv_b, ssd_dt_bias, ssd_a_log, ssd_d, ssd_norm,
           fox_bf, s5_lam_re, s5_lam_im, s5_b_re, s5_b_im, s5_c_re, s5_c_im, s5_log_step, s5_d,
           s5_w_glu, w_branch, w_out, norm2, w_ffn_in, w_ffn_out, norm_f):
    batch, seq, _ = x.shape
    length = seq + N_META
    granule = math.lcm(CHUNK, ATT_TILE)
    lp = -(-length // granule) * granule
    assert batch % 8 == 0 and (lp * batch) % ROW_TILE == 0 and lp % S5_STEPS == 0
    rows = lp * batch

    xt = jnp.concatenate([
        jnp.broadcast_to(meta[None].astype(F32), (batch, N_META, D_MODEL)),
        x.astype(F32),
        jnp.zeros((batch, lp - length, D_MODEL), F32)], axis=1).reshape(rows, D_MODEL)

    bblk, a_re, a_im, cblk = _prep_s5(s5_lam_re, s5_lam_im, s5_b_re, s5_b_im, s5_c_re, s5_c_im,
                                      s5_log_step)
    row3 = lambda v: v.astype(F32)[:, None, :]
    layers = dict(
        norm1=row3(norm1), w_in=_prep_w_in(w_in),
        conv_w=ssd_conv_w.astype(F32), conv_b=row3(ssd_conv_b),
        dt_bias=_pad_lanes(ssd_dt_bias), a_log=_pad_lanes(ssd_a_log),
        ssd_d=row3(jnp.repeat(ssd_d, SSD_HEAD_DIM, axis=-1)), ssd_norm=row3(ssd_norm),
        fox_bf=_pad_lanes(fox_bf, F_LANE0),
        bblk=bblk, a_re=a_re, a_im=a_im, cblk=cblk, s5_d=row3(s5_d), w_glu=s5_w_glu.astype(BF16),
        w_branch=w_branch.astype(BF16), w_out=w_out.astype(BF16), norm2=row3(norm2),
        w_ffn_in=w_ffn_in.astype(BF16), w_ffn_out=w_ffn_out.astype(BF16),
    )

    def layer(xr, p):
        z, xbc, qkv, u, gate, small = _inproj(xr, p['norm1'], p['w_in'])
        seqv = lambda a: a.reshape(batch, lp, a.shape[1])
        small_s = seqv(small)
        y_a = _ssd(seqv(z), seqv(xbc), small_s, p['conv_w'], p['conv_b'], p['dt_bias'], p['a_log'],
                   p['ssd_d'], p['ssd_norm'], batch=batch).reshape(rows, D_MODEL)
        cum, cum_t = _fox_cum(small_s, p['fox_bf'], batch=batch)
        y_b = _attention(seqv(qkv), cum, cum_t, batch=batch).reshape(rows, D_MODEL)
        y_c = _s5(seqv(u), p['bblk'], p['a_re'], p['a_im'], p['cblk'], p['s5_d'],
                  p['w_glu']).reshape(rows, D_MODEL)
        x1 = _merge(y_a, y_b, y_c, gate, xr, p['w_branch'], p['w_out'])
        return _ffn(x1, p['norm2'], p['w_ffn_in'], p['w_ffn_out']), None

    xr, _ = lax.scan(layer, xt, layers)
    out = _final_norm(xr, norm_f.astype(F32)[None, :]).reshape(batch, lp, D_MODEL)
    return out[:, N_META:length]
```

```python
import functools

import jax
import jax.numpy as jnp
from jax import lax
from jax.experimental import pallas as pl
from jax.experimental.pallas import tpu as pltpu

F32 = jnp.float32
BF16 = jnp.bfloat16

D_MODEL = 1024
N_META = 16
CHUNK = 128
EPS = 1e-6
NEG = -1e30

SSD_HEADS = 16
SSD_HEAD_DIM = 64
SSD_GROUPS = 2
SSD_STATE = 128
SSD_CONV = 4
SSD_CONV_DIM = D_MODEL + 2 * SSD_GROUPS * SSD_STATE

FOX_HEADS = 8
FOX_HEAD_DIM = 128

S5_GROUP = 16
S5_GROUPS = D_MODEL // S5_GROUP
S5_STATE = 64
S5_BLOCKS = 8
S5_BLOCK_GROUPS = S5_GROUPS // S5_BLOCKS
S5_HALF = S5_BLOCK_GROUPS * S5_STATE

D_FF = 2816
SMALL_W = 128
F_LANE0 = SSD_HEADS

SEC_WIDTHS = (D_MODEL, SSD_CONV_DIM, 3 * D_MODEL, D_MODEL, 3 * D_MODEL, SMALL_W)
D_IN_P = sum(SEC_WIDTHS)

ROW_TILE = 512
COL_CHUNK = 512
ATT_TILE = 512
S5_STEPS = 16
VMEM_LIMIT = 56 * 1024 * 1024


def _cparams(sems):
    return pltpu.CompilerParams(dimension_semantics=sems, vmem_limit_bytes=VMEM_LIMIT)


def _resident(shape):
    nd = len(shape)
    return pl.BlockSpec(shape, lambda *_: (0,) * nd, pipeline_mode=pl.Buffered(1))


def _sigmoid(x):
    return 1.0 / (1.0 + jnp.exp(-x))


def _softplus(x):
    return jnp.maximum(x, 0.0) + jnp.log(1.0 + jnp.exp(-jnp.abs(x)))


def _bf16_part(x):
    bits = lax.bitcast_convert_type(x, jnp.uint32) & jnp.uint32(0xFFFF0000)
    return lax.bitcast_convert_type(bits, F32)


def _rms(x, w):
    ms = jnp.mean(x * x, axis=-1, keepdims=True)
    return (x * lax.rsqrt(ms + EPS)) * w


def _inproj_kernel(x_ref, nw_ref, w_ref, *out_refs):
    xn = _rms(x_ref[...], nw_ref[...]).astype(BF16)
    off = 0
    for ref in out_refs:
        width = ref.shape[1]
        for c0 in range(0, width, COL_CHUNK):
            cw = min(COL_CHUNK, width - c0)
            ref[:, c0:c0 + cw] = jnp.dot(
                xn, w_ref[:, off + c0:off + c0 + cw], preferred_element_type=F32
            ).astype(ref.dtype)
        off += width


def _inproj(x, nw, w):
    rows = x.shape[0]
    dtypes = (BF16, BF16, BF16, BF16, BF16, F32)
    return pl.pallas_call(
        _inproj_kernel,
        grid=(rows // ROW_TILE,),
        in_specs=[
            pl.BlockSpec((ROW_TILE, D_MODEL), lambda i: (i, 0)),
            _resident((1, D_MODEL)),
            _resident((D_MODEL, D_IN_P)),
        ],
        out_specs=[pl.BlockSpec((ROW_TILE, wd), lambda i: (i, 0)) for wd in SEC_WIDTHS],
        out_shape=[jax.ShapeDtypeStruct((rows, wd), dt) for wd, dt in zip(SEC_WIDTHS, dtypes)],
        compiler_params=_cparams(("parallel",)),
        name="inproj",
    )(x, nw, w)


def _ssd_consts():
    t = jnp.arange(CHUNK)
    cols = jnp.arange(2 * CHUNK)
    shift = jnp.concatenate(
        [(cols[None, :] == (CHUNK + t - k)[:, None]) for k in range(1, SSD_CONV)], axis=0)
    head_of_lane = jnp.arange(D_MODEL) // SSD_HEAD_DIM
    expand = jnp.arange(SMALL_W)[:, None] == head_of_lane[None, :]
    return shift.astype(BF16), expand.astype(BF16), expand.astype(F32)


def _ssd_kernel(z_ref, xbc_ref, sm_ref, cw_ref, cb_ref, dtb_ref, alog_ref, bf_ref, dsk_ref, nw_ref,
                shift_ref, exp_ref, exp32_ref, sel_ref, ones_ref,
                o_ref, ak_ref, aq_ref, prev_sc, state_sc, fcarry_sc):
    c = pl.program_id(1)

    @pl.when(c == 0)
    def _():
        prev_sc[...] = jnp.zeros_like(prev_sc)
        state_sc[...] = jnp.zeros_like(state_sc)
        fcarry_sc[...] = jnp.zeros_like(fcarry_sc)

    cur_bf = xbc_ref[...]
    both = jnp.concatenate([prev_sc[(c + 1) % 2], cur_bf], axis=0)
    prev_sc[c % 2] = cur_bf
    shifted = jnp.dot(shift_ref[...], both, preferred_element_type=F32)
    conv = cb_ref[...] + cw_ref[SSD_CONV - 1:SSD_CONV, :] * cur_bf.astype(F32)
    for k in range(1, SSD_CONV):
        conv = conv + (cw_ref[SSD_CONV - 1 - k:SSD_CONV - k, :]
                       * shifted[(k - 1) * CHUNK:k * CHUNK])
    xc = conv * _sigmoid(conv)
    xs = xc[:, :D_MODEL]
    n_bc = SSD_GROUPS * SSD_STATE
    bm = xc[:, D_MODEL:D_MODEL + n_bc]
    cm = xc[:, D_MODEL + n_bc:]

    sm = sm_ref[...]
    lane = lax.broadcasted_iota(jnp.int32, (CHUNK, SMALL_W), 1)
    dt = _softplus(sm + dtb_ref[...])
    xf = sm + bf_ref[...]
    logf = jnp.minimum(xf, 0.0) - jnp.log(1.0 + jnp.exp(-jnp.abs(xf)))
    is_ssd = lane < SSD_HEADS
    row = lax.broadcasted_iota(jnp.int32, (CHUNK, CHUNK), 0)
    col = lax.broadcasted_iota(jnp.int32, (CHUNK, CHUNK), 1)
    causal = row >= col
    tri = jnp.where(causal, 1.0, 0.0).astype(F32)
    cs = jnp.dot(tri, jnp.where(is_ssd, dt * (-jnp.exp(alog_ref[...])), logf),
                 preferred_element_type=F32, precision=lax.Precision.HIGHEST)

    fcum = cs + fcarry_sc[...]
    fcarry_sc[...] = jnp.where(is_ssd[:1], 0.0, fcum[CHUNK - 1:CHUNK, :])
    c2 = fcum * LOG2E
    hi = _bf16_part(c2)
    mid = _bf16_part(c2 - hi)
    lo = (c2 - hi) - mid
    pieces = jnp.concatenate([hi, mid, lo], axis=1).astype(BF16)
    aug = jnp.dot(pieces, sel_ref[...], preferred_element_type=F32) + ones_ref[...]
    ak_ref[...] = aug[:, :SMALL_W].astype(BF16)
    aq_ref[...] = aug[:, SMALL_W:].astype(BF16)

    acs = cs
    alast = acs[CHUNK - 1:CHUNK, :]
    acs_t = acs.T
    per_head = jnp.concatenate([dt, jnp.exp(acs), jnp.exp(alast - acs) * dt], axis=0)
    expd = jnp.dot(per_head.astype(BF16), exp_ref[...], preferred_element_type=F32)
    dt_x, eacs_x, w_x = expd[:CHUNK], expd[CHUNK:2 * CHUNK], expd[2 * CHUNK:]
    dec_x = jnp.dot(jnp.broadcast_to(jnp.exp(alast), (8, SMALL_W)), exp32_ref[...],
                    preferred_element_type=F32, precision=lax.Precision.HIGHEST)[:1]
    xdt = (xs * dt_x).astype(BF16)
    xw = (xs * w_x).astype(BF16)
    state = state_sc[...]
    state_bf = state.astype(BF16)

    half = lax.broadcasted_iota(jnp.int32, (CHUNK, 2 * SSD_HEAD_DIM), 1) < SSD_HEAD_DIM
    hpg = SSD_HEADS // SSD_GROUPS
    gw = hpg * SSD_HEAD_DIM
    y_diag, y_off, s_new = [], [], []
    for g in range(SSD_GROUPS):
        bg = bm[:, g * SSD_STATE:(g + 1) * SSD_STATE]
        cg = cm[:, g * SSD_STATE:(g + 1) * SSD_STATE].astype(BF16)
        cb = lax.dot_general(cg, bg.astype(BF16), (((1,), (1,)), ((), ())),
                             preferred_element_type=F32)
        y_off.append(jnp.dot(cg, state_bf[:, g * gw:(g + 1) * gw], preferred_element_type=F32))
        s_new.append(jnp.dot(bg.T.astype(BF16), xw[:, g * gw:(g + 1) * gw],
                             preferred_element_type=F32))
        for pair in range(hpg // 2):
            gm = []
            for h in (g * hpg + 2 * pair, g * hpg + 2 * pair + 1):
                lmat = jnp.where(causal, jnp.exp(acs[:, h:h + 1] - acs_t[h:h + 1, :]), 0.0)
                gm.append((cb * lmat).astype(BF16))
            k0 = (g * hpg + 2 * pair) * SSD_HEAD_DIM
            xp = xdt[:, k0:k0 + 2 * SSD_HEAD_DIM]
            zero = jnp.zeros_like(xp)
            rhs = jnp.concatenate([jnp.where(half, xp, zero), jnp.where(half, zero, xp)], axis=0)
            y_diag.append(jnp.dot(jnp.concatenate(gm, axis=1), rhs, preferred_element_type=F32))

    state_sc[...] = state * dec_x + jnp.concatenate(s_new, axis=1)
    y = (jnp.concatenate(y_diag, axis=1) + jnp.concatenate(y_off, axis=1) * eacs_x
         + xs * dsk_ref[...])
    zf = z_ref[...].astype(F32)
    y = y * (zf * _sigmoid(zf))
    o_ref[...] = _rms(y, nw_ref[...]).astype(o_ref.dtype)


def _ssd(z, xbc, small, cw, cb, dtb, alog, bf, dsk, nw, *, batch):
    lp = z.shape[1]
    seq_blk = lambda w: pl.BlockSpec((None, CHUNK, w), lambda b, c: (b, c, 0))
    consts = _ssd_consts() + _aug_select()
    return pl.pallas_call(
        _ssd_kernel,
        grid=(batch, lp // CHUNK),
        in_specs=[
            seq_blk(D_MODEL), seq_blk(SSD_CONV_DIM), seq_blk(SMALL_W),
            _resident((SSD_CONV, SSD_CONV_DIM)), _resident((1, SSD_CONV_DIM)),
            _resident((1, SMALL_W)), _resident((1, SMALL_W)), _resident((1, SMALL_W)),
            _resident((1, D_MODEL)), _resident((1, D_MODEL)),
        ] + [_resident(a.shape) for a in consts],
        out_specs=[seq_blk(D_MODEL), seq_blk(SMALL_W), seq_blk(SMALL_W)],
        out_shape=[jax.ShapeDtypeStruct((batch, lp, D_MODEL), BF16),
                   jax.ShapeDtypeStruct((batch, lp, SMALL_W), BF16),
                   jax.ShapeDtypeStruct((batch, lp, SMALL_W), BF16)],
        scratch_shapes=[
            pltpu.VMEM((2, CHUNK, SSD_CONV_DIM), BF16),
            pltpu.VMEM((SSD_STATE, D_MODEL), F32),
            pltpu.VMEM((1, SMALL_W), F32),
        ],
        compiler_params=_cparams(("parallel", "arbitrary")),
        name="ssd",
    )(z, xbc, small, cw, cb, dtb, alog, bf, dsk, nw, *consts)


AUG_LANES = 6
LOG2E = 1.4426950408889634


def _aug_select():
    sel = [[0.0] * (2 * SMALL_W) for _ in range(3 * SMALL_W)]
    ones = [0.0] * (2 * SMALL_W)
    for h in range(FOX_HEADS):
        for p in range(3):
            sel[p * SMALL_W + F_LANE0 + h][AUG_LANES * h + p] = -1.0
            sel[p * SMALL_W + F_LANE0 + h][SMALL_W + AUG_LANES * h + 3 + p] = 1.0
            ones[AUG_LANES * h + 3 + p] = 1.0
            ones[SMALL_W + AUG_LANES * h + p] = 1.0
    return jnp.array(sel, BF16), jnp.array([ones], F32)


def _attn_kernel(q_ref, k_ref, v_ref, ak_ref, aq_ref, o_ref, vt_sc, m_sc, l_sc, acc_sc, s_sc,
                 *, lp):
    tq = ATT_TILE
    hd = FOX_HEAD_DIM
    i = pl.program_id(1)
    n_full = lp // tq
    tail = lp - n_full * tq
    heads = range(FOX_HEADS)

    @pl.when(i == 0)
    def _():
        def tr(c, _):
            r = pl.multiple_of(c * CHUNK, CHUNK)
            for h in heads:
                vt_sc[h, :, pl.ds(r, CHUNK)] = (
                    v_ref[pl.ds(r, CHUNK), h * hd:(h + 1) * hd].astype(F32).T.astype(BF16))
            return 0
        lax.fori_loop(0, lp // CHUNK, tr, 0)

    lane = lax.broadcasted_iota(jnp.int32, (tq, SMALL_W), 1)
    aq = aq_ref[...]
    qf = []
    for h in heads:
        mine = (lane >= AUG_LANES * h) & (lane < AUG_LANES * (h + 1))
        qf.append(jnp.concatenate([q_ref[:, h * hd:(h + 1) * hd],
                                   jnp.where(mine, aq, jnp.zeros_like(aq))], axis=1))
        m_sc[h] = jnp.full((1, tq), NEG, F32)
        l_sc[h] = jnp.zeros((1, tq), F32)
        acc_sc[h] = jnp.zeros((hd, tq), F32)

    def block(ks, tk, masked):
        akb = ak_ref[pl.ds(ks, tk), :]
        if masked:
            row = lax.broadcasted_iota(jnp.int32, (tk, tq), 0)
            col = lax.broadcasted_iota(jnp.int32, (tk, tq), 1)
            keep = row <= col

        def scores(h):
            kb = jnp.concatenate([k_ref[pl.ds(ks, tk), h * hd:(h + 1) * hd], akb], axis=1)
            s_sc[h % 2, :tk, :] = lax.dot_general(
                kb, qf[h], (((1,), (1,)), ((), ())), preferred_element_type=F32)

        scores(0)
        for h in heads:
            if h + 1 < FOX_HEADS:
                scores(h + 1)
            s = s_sc[h % 2, :tk, :]
            if masked:
                s = jnp.where(keep, s, NEG)
            m = m_sc[h]
            m_new = jnp.maximum(m, jnp.max(s, axis=0, keepdims=True))
            p = jnp.exp2(s - m_new)
            alpha = jnp.exp2(m - m_new)
            m_sc[h] = m_new
            l_sc[h] = alpha * l_sc[h] + jnp.sum(p, axis=0, keepdims=True)
            acc_sc[h] = alpha * acc_sc[h] + jnp.dot(
                vt_sc[h, :, pl.ds(ks, tk)], p.astype(BF16), preferred_element_type=F32)

    def full_block(kj, _):
        block(pl.multiple_of(kj * tq, tq), tq, False)
        return 0

    lax.fori_loop(0, i, full_block, 0)
    diag = pl.multiple_of(i * tq, tq)

    def finish(tk):
        block(diag, tk, True)
        for h in heads:
            o_ref[:, h * hd:(h + 1) * hd] = (acc_sc[h] / l_sc[h]).T.astype(o_ref.dtype)

    if tail:
        pl.when(i < n_full)(lambda: finish(tq))
        pl.when(i == n_full)(lambda: finish(tail))
    else:
        finish(tq)


def _attention(qkv, ak, aq, *, batch):
    lp = qkv.shape[1]
    hd = FOX_HEAD_DIM
    nh = FOX_HEADS
    width = nh * hd
    once = dict(pipeline_mode=pl.Buffered(1))
    return pl.pallas_call(
        functools.partial(_attn_kernel, lp=lp),
        grid=(batch, pl.cdiv(lp, ATT_TILE)),
        in_specs=[
            pl.BlockSpec((None, ATT_TILE, width), lambda b, i: (b, i, 0)),
            pl.BlockSpec((None, lp, width), lambda b, i: (b, 0, 1), **once),
            pl.BlockSpec((None, lp, width), lambda b, i: (b, 0, 2), **once),
            pl.BlockSpec((None, lp, SMALL_W), lambda b, i: (b, 0, 0), **once),
            pl.BlockSpec((None, ATT_TILE, SMALL_W), lambda b, i: (b, i, 0)),
        ],
        out_specs=pl.BlockSpec((None, ATT_TILE, width), lambda b, i: (b, i, 0)),
        out_shape=jax.ShapeDtypeStruct((batch, lp, width), BF16),
        scratch_shapes=[
            pltpu.VMEM((nh, hd, lp), BF16),
            pltpu.VMEM((nh, 1, ATT_TILE), F32),
            pltpu.VMEM((nh, 1, ATT_TILE), F32),
            pltpu.VMEM((nh, hd, ATT_TILE), F32),
            pltpu.VMEM((2, ATT_TILE, ATT_TILE), F32),
        ],
        compiler_params=_cparams(("parallel", "arbitrary")),
        name="fox_attn",
    )(qkv, qkv, qkv, ak, aq)


def _s5_kernel(u_ref, perm_ref, perm_t_ref, bblk_ref, are_ref, aim_ref, cblk_ref, dsk_ref,
               wglu_ref, o_ref, h_sc, carry_sc, y_sc, *, batch):
    blk = 2 * S5_HALF

    @pl.when(pl.program_id(0) == 0)
    def _():
        carry_sc[...] = jnp.zeros_like(carry_sc)

    u_bt = u_ref[...].reshape(batch * S5_STEPS, D_MODEL)
    u = jnp.dot(perm_ref[...], u_bt, preferred_element_type=F32).astype(BF16)
    for j in range(S5_BLOCKS):
        h_sc[:, j * blk:(j + 1) * blk] = jnp.dot(
            u[:, j * 128:(j + 1) * 128], bblk_ref[j], preferred_element_type=F32)

    for j in range(S5_BLOCKS):
        re_l = slice(j * blk, j * blk + S5_HALF)
        im_l = slice(j * blk + S5_HALF, (j + 1) * blk)
        a_re = are_ref[:, j * S5_HALF:(j + 1) * S5_HALF]
        a_im = aim_ref[:, j * S5_HALF:(j + 1) * S5_HALF]
        h_re = carry_sc[:, re_l]
        h_im = carry_sc[:, im_l]
        for t in range(S5_STEPS):
            rows_t = slice(t * batch, (t + 1) * batch)
            n_re = a_re * h_re - a_im * h_im + h_sc[rows_t, re_l]
            n_im = a_re * h_im + a_im * h_re + h_sc[rows_t, im_l]
            h_sc[rows_t, re_l] = n_re
            h_sc[rows_t, im_l] = n_im
            h_re, h_im = n_re, n_im
        carry_sc[:, re_l] = h_re
        carry_sc[:, im_l] = h_im

    for j in range(S5_BLOCKS):
        y_sc[:, j * 128:(j + 1) * 128] = jnp.dot(
            h_sc[:, j * blk:(j + 1) * blk].astype(BF16), cblk_ref[j], preferred_element_type=F32)

    y = y_sc[...] + dsk_ref[...] * u.astype(F32)
    y = 0.5 * y * (1.0 + jnp.tanh(0.7978845608028654 * (y + 0.044715 * (y * y * y))))
    gl = jnp.dot(y.astype(BF16), wglu_ref[...], preferred_element_type=F32)
    out_tb = (y * _sigmoid(gl)).astype(BF16)
    out_bt = jnp.dot(perm_t_ref[...], out_tb, preferred_element_type=F32).astype(o_ref.dtype)
    o_ref[...] = out_bt.reshape(batch, S5_STEPS, D_MODEL)


def _s5(u, bblk, a_re, a_im, cblk, dsk, wglu):
    batch, lp, _ = u.shape
    rows = S5_STEPS * batch
    n_state = S5_BLOCKS * 2 * S5_HALF
    tblk = pl.BlockSpec((batch, S5_STEPS, D_MODEL), lambda i: (0, i, 0))
    src = (jnp.arange(rows) % batch) * S5_STEPS + jnp.arange(rows) // batch
    perm = (src[:, None] == jnp.arange(rows)[None, :]).astype(BF16)
    return pl.pallas_call(
        functools.partial(_s5_kernel, batch=batch),
        grid=(lp // S5_STEPS,),
        in_specs=[
            tblk,
            _resident((rows, rows)), _resident((rows, rows)),
            _resident((S5_BLOCKS, 128, 2 * S5_HALF)),
            _resident((1, S5_BLOCKS * S5_HALF)), _resident((1, S5_BLOCKS * S5_HALF)),
            _resident((S5_BLOCKS, 2 * S5_HALF, 128)),
            _resident((1, D_MODEL)),
            _resident((D_MODEL, D_MODEL)),
        ],
        out_specs=tblk,
        out_shape=jax.ShapeDtypeStruct((batch, lp, D_MODEL), BF16),
        scratch_shapes=[
            pltpu.VMEM((rows, n_state), F32),
            pltpu.VMEM((batch, n_state), F32),
            pltpu.VMEM((rows, D_MODEL), F32),
        ],
        compiler_params=_cparams(("arbitrary",)),
        name="s5",
    )(u, perm, perm.T, bblk, a_re, a_im, cblk, dsk, wglu)


def _merge_kernel(ya_ref, yb_ref, yc_ref, gate_ref, x_ref, wb_ref, wo_ref, o_ref):
    acc = None
    for n, y_ref in enumerate((ya_ref, yb_ref, yc_ref)):
        br = jnp.dot(y_ref[...], wb_ref[n], preferred_element_type=F32)
        gt = _sigmoid(gate_ref[:, n * D_MODEL:(n + 1) * D_MODEL].astype(F32))
        acc = gt * br if acc is None else acc + gt * br
    o_ref[...] = x_ref[...] + jnp.dot(acc.astype(BF16), wo_ref[...], preferred_element_type=F32)


def _merge(ya, yb, yc, gate, x, wb, wo):
    rows = x.shape[0]
    rblk = lambda w: pl.BlockSpec((ROW_TILE, w), lambda i: (i, 0))
    return pl.pallas_call(
        _merge_kernel,
        grid=(rows // ROW_TILE,),
        in_specs=[rblk(D_MODEL), rblk(D_MODEL), rblk(D_MODEL), rblk(3 * D_MODEL), rblk(D_MODEL),
                  _resident((3, D_MODEL, D_MODEL)), _resident((D_MODEL, D_MODEL))],
        out_specs=rblk(D_MODEL),
        out_shape=jax.ShapeDtypeStruct((rows, D_MODEL), F32),
        compiler_params=_cparams(("parallel",)),
        name="merge",
    )(ya, yb, yc, gate, x, wb, wo)


def _ffn_kernel(x_ref, nw_ref, w1_ref, w2_ref, o_ref):
    x = x_ref[...]
    xn = _rms(x, nw_ref[...]).astype(BF16)
    half = D_FF // 2
    acc = x
    for c0 in (0, half):
        gp = jnp.dot(xn, w1_ref[:, c0:c0 + half], preferred_element_type=F32)
        up = jnp.dot(xn, w1_ref[:, D_FF + c0:D_FF + c0 + half], preferred_element_type=F32)
        act = (gp * _sigmoid(gp) * up).astype(BF16)
        acc = acc + jnp.dot(act, w2_ref[c0:c0 + half, :], preferred_element_type=F32)
    o_ref[...] = acc


def _ffn(x, nw, w1, w2):
    rows = x.shape[0]
    return pl.pallas_call(
        _ffn_kernel,
        grid=(rows // ROW_TILE,),
        in_specs=[pl.BlockSpec((ROW_TILE, D_MODEL), lambda i: (i, 0)), _resident((1, D_MODEL)),
                  _resident((D_MODEL, 2 * D_FF)), _resident((D_FF, D_MODEL))],
        out_specs=pl.BlockSpec((ROW_TILE, D_MODEL), lambda i: (i, 0)),
        out_shape=jax.ShapeDtypeStruct((rows, D_MODEL), F32),
        compiler_params=_cparams(("parallel",)),
        name="ffn",
    )(x, nw, w1, w2)


def _final_norm_kernel(x_ref, nw_ref, o_ref):
    o_ref[...] = _rms(x_ref[...], nw_ref[...])


def _final_norm(x, nw):
    rows = x.shape[0]
    return pl.pallas_call(
        _final_norm_kernel,
        grid=(rows // ROW_TILE,),
        in_specs=[pl.BlockSpec((ROW_TILE, D_MODEL), lambda i: (i, 0)), _resident((1, D_MODEL))],
        out_specs=pl.BlockSpec((ROW_TILE, D_MODEL), lambda i: (i, 0)),
        out_shape=jax.ShapeDtypeStruct((rows, D_MODEL), F32),
        compiler_params=_cparams(("parallel",)),
        name="final_norm",
    )(x, nw)


def _pad_lanes(v, lane0=0):
    depth, n = v.shape
    out = jnp.zeros((depth, 1, SMALL_W), F32)
    return out.at[:, 0, lane0:lane0 + n].set(v.astype(F32))


def _prep_w_in(w_in):
    o = [0, D_MODEL, D_MODEL + SSD_CONV_DIM]
    o.append(o[-1] + SSD_HEADS)
    o.append(o[-1] + 3 * D_MODEL)
    o.append(o[-1] + FOX_HEADS)
    o.append(o[-1] + D_MODEL)
    o.append(o[-1] + 3 * D_MODEL)
    z, xbc, dt, qkv, fr, u, gate = [w_in[..., o[k]:o[k + 1]] for k in range(7)]
    pad = jnp.zeros(w_in.shape[:-1] + (SMALL_W - SSD_HEADS - FOX_HEADS,), w_in.dtype)
    q_scale = jnp.concatenate([jnp.full((D_MODEL,), FOX_HEAD_DIM ** -0.5 * LOG2E, w_in.dtype),
                               jnp.ones((2 * D_MODEL,), w_in.dtype)])
    return jnp.concatenate([z, xbc, qkv * q_scale, u, gate, dt, fr, pad], axis=-1).astype(BF16)


def _prep_s5(lam_re, lam_im, b_re, b_im, c_re, c_im, log_step):
    depth = lam_re.shape[0]
    lam = lax.complex(lam_re.astype(F32), lam_im.astype(F32))
    step = jnp.exp(log_step.astype(F32))[..., None]
    lam_bar = jnp.exp(lam * step)
    b_bar = ((lam_bar - 1.0) / lam)[..., None] * lax.complex(b_re.astype(F32), b_im.astype(F32))
    nb, gb, p, ch = S5_BLOCKS, S5_BLOCK_GROUPS, S5_STATE, S5_GROUP
    eye = jnp.eye(gb, dtype=F32)

    def b_block(part):
        part = part.reshape(depth, nb, gb, p, ch)
        m = jnp.einsum('djgpc,gh->djgchp', part, eye)
        return m.reshape(depth, nb, gb * ch, gb * p)

    bblk = jnp.concatenate([b_block(jnp.real(b_bar)), b_block(jnp.imag(b_bar))], axis=-1)

    def c_block(part):
        part = part.reshape(depth, nb, gb, ch, p)
        m = jnp.einsum('djgcp,gh->djgphc', part, eye)
        return m.reshape(depth, nb, gb * p, gb * ch)

    cblk = jnp.concatenate([c_block(c_re.astype(F32)), c_block(-c_im.astype(F32))], axis=-2)
    a_re = jnp.real(lam_bar).reshape(depth, 1, nb * gb * p)
    a_im = jnp.imag(lam_bar).reshape(depth, 1, nb * gb * p)
    return bblk.astype(BF16), a_re, a_im, cblk.astype(BF16)


def kernel(x, meta, norm1, w_in, ssd_conv_w, ssd_conv_b, ssd_dt_bias, ssd_a_log, ssd_d, ssd_norm,
           fox_bf, s5_lam_re, s5_lam_im, s5_b_re, s5_b_im, s5_c_re, s5_c_im, s5_log_step, s5_d,
           s5_w_glu, w_branch, w_out, norm2, w_ffn_in, w_ffn_out, norm_f):
    batch, seq, _ = x.shape
    length = seq + N_META
    lp = -(-length // CHUNK) * CHUNK
    assert batch % 8 == 0 and (lp * batch) % ROW_TILE == 0 and lp % S5_STEPS == 0
    rows = lp * batch

    xt = jnp.concatenate([
        jnp.broadcast_to(meta[None].astype(F32), (batch, N_META, D_MODEL)),
        x.astype(F32),
        jnp.zeros((batch, lp - length, D_MODEL), F32)], axis=1).reshape(rows, D_MODEL)

    bblk, a_re, a_im, cblk = _prep_s5(s5_lam_re, s5_lam_im, s5_b_re, s5_b_im, s5_c_re, s5_c_im,
                                      s5_log_step)
    row3 = lambda v: v.astype(F32)[:, None, :]
    layers = dict(
        norm1=row3(norm1), w_in=_prep_w_in(w_in),
        conv_w=ssd_conv_w.astype(F32), conv_b=row3(ssd_conv_b),
        dt_bias=_pad_lanes(ssd_dt_bias), a_log=_pad_lanes(ssd_a_log),
        ssd_d=row3(jnp.repeat(ssd_d, SSD_HEAD_DIM, axis=-1)), ssd_norm=row3(ssd_norm),
        fox_bf=_pad_lanes(fox_bf, F_LANE0),
        bblk=bblk, a_re=a_re, a_im=a_im, cblk=cblk, s5_d=row3(s5_d), w_glu=s5_w_glu.astype(BF16),
        w_branch=w_branch.astype(BF16), w_out=w_out.astype(BF16), norm2=row3(norm2),
        w_ffn_in=w_ffn_in.astype(BF16), w_ffn_out=w_ffn_out.astype(BF16),
    )

    def layer(xr, p):
        z, xbc, qkv, u, gate, small = _inproj(xr, p['norm1'], p['w_in'])
        seqv = lambda a: a.reshape(batch, lp, a.shape[1])
        small_s = seqv(small)
        y_a, ak, aq = _ssd(seqv(z), seqv(xbc), small_s, p['conv_w'], p['conv_b'], p['dt_bias'],
                           p['a_log'], p['fox_bf'], p['ssd_d'], p['ssd_norm'], batch=batch)
        y_a = y_a.reshape(rows, D_MODEL)
        y_b = _attention(seqv(qkv), ak, aq, batch=batch).reshape(rows, D_MODEL)
        y_c = _s5(seqv(u), p['bblk'], p['a_re'], p['a_im'], p['cblk'], p['s5_d'],
                  p['w_glu']).reshape(rows, D_MODEL)
        x1 = _merge(y_a, y_b, y_c, gate, xr, p['w_branch'], p['w_out'])
        return _ffn(x1, p['norm2'], p['w_ffn_in'], p['w_ffn_out']), None

    xr, _ = lax.scan(layer, xt, layers)
    out = _final_norm(xr, norm_f.astype(F32)[None, :]).reshape(batch, lp, D_MODEL)
    return out[:, N_META:length]
```

```python
import functools

import jax
import jax.numpy as jnp
from jax import lax
from jax.experimental import pallas as pl
from jax.experimental.pallas import tpu as pltpu

F32 = jnp.float32
BF16 = jnp.bfloat16

D_MODEL = 1024
N_META = 16
CHUNK = 128
EPS = 1e-6
NEG = -1e30

SSD_HEADS = 16
SSD_HEAD_DIM = 64
SSD_GROUPS = 2
SSD_STATE = 128
SSD_CONV = 4
SSD_CONV_DIM = D_MODEL + 2 * SSD_GROUPS * SSD_STATE

FOX_HEADS = 8
FOX_HEAD_DIM = 128

S5_GROUP = 16
S5_GROUPS = D_MODEL // S5_GROUP
S5_STATE = 64
S5_BLOCKS = 8
S5_BLOCK_GROUPS = S5_GROUPS // S5_BLOCKS
S5_HALF = S5_BLOCK_GROUPS * S5_STATE

D_FF = 2816
SMALL_W = 128
F_LANE0 = SSD_HEADS

SEC_WIDTHS = (D_MODEL, SSD_CONV_DIM, 3 * D_MODEL, D_MODEL, 3 * D_MODEL, SMALL_W)
D_IN_P = sum(SEC_WIDTHS)

ROW_TILE = 512
COL_CHUNK = 512
ATT_TILE = 512
ATT_AHEAD = 3
S5_STEPS = 16
VMEM_LIMIT = 56 * 1024 * 1024


def _cparams(sems):
    return pltpu.CompilerParams(dimension_semantics=sems, vmem_limit_bytes=VMEM_LIMIT)


def _resident(shape):
    nd = len(shape)
    return pl.BlockSpec(shape, lambda *_: (0,) * nd, pipeline_mode=pl.Buffered(1))


def _sigmoid(x):
    return 1.0 / (1.0 + jnp.exp(-x))


def _softplus(x):
    return jnp.maximum(x, 0.0) + jnp.log(1.0 + jnp.exp(-jnp.abs(x)))


def _bf16_part(x):
    bits = lax.bitcast_convert_type(x, jnp.uint32) & jnp.uint32(0xFFFF0000)
    return lax.bitcast_convert_type(bits, F32)


def _rms(x, w):
    ms = jnp.mean(x * x, axis=-1, keepdims=True)
    return (x * lax.rsqrt(ms + EPS)) * w


def _inproj_kernel(x_ref, nw_ref, w_ref, *out_refs):
    xn = _rms(x_ref[...], nw_ref[...]).astype(BF16)
    off = 0
    for ref in out_refs:
        width = ref.shape[1]
        for c0 in range(0, width, COL_CHUNK):
            cw = min(COL_CHUNK, width - c0)
            ref[:, c0:c0 + cw] = jnp.dot(
                xn, w_ref[:, off + c0:off + c0 + cw], preferred_element_type=F32
            ).astype(ref.dtype)
        off += width


def _inproj(x, nw, w):
    rows = x.shape[0]
    dtypes = (BF16, BF16, BF16, BF16, BF16, F32)
    return pl.pallas_call(
        _inproj_kernel,
        grid=(rows // ROW_TILE,),
        in_specs=[
            pl.BlockSpec((ROW_TILE, D_MODEL), lambda i: (i, 0)),
            _resident((1, D_MODEL)),
            _resident((D_MODEL, D_IN_P)),
        ],
        out_specs=[pl.BlockSpec((ROW_TILE, wd), lambda i: (i, 0)) for wd in SEC_WIDTHS],
        out_shape=[jax.ShapeDtypeStruct((rows, wd), dt) for wd, dt in zip(SEC_WIDTHS, dtypes)],
        compiler_params=_cparams(("parallel",)),
        name="inproj",
    )(x, nw, w)


def _ssd_consts():
    t = jnp.arange(CHUNK)
    cols = jnp.arange(2 * CHUNK)
    shift = jnp.concatenate(
        [(cols[None, :] == (CHUNK + t - k)[:, None]) for k in range(1, SSD_CONV)], axis=0)
    head_of_lane = jnp.arange(D_MODEL) // SSD_HEAD_DIM
    expand = jnp.arange(SMALL_W)[:, None] == head_of_lane[None, :]
    return shift.astype(BF16), expand.astype(BF16), expand.astype(F32)


def _ssd_kernel(z_ref, xbc_ref, sm_ref, cw_ref, cb_ref, dtb_ref, alog_ref, bf_ref, dsk_ref, nw_ref,
                shift_ref, exp_ref, exp32_ref, sel_ref, ones_ref,
                o_ref, ak_ref, aq_ref, prev_sc, state_sc, fcarry_sc):
    c = pl.program_id(1)

    @pl.when(c == 0)
    def _():
        prev_sc[...] = jnp.zeros_like(prev_sc)
        state_sc[...] = jnp.zeros_like(state_sc)
        fcarry_sc[...] = jnp.zeros_like(fcarry_sc)

    cur_bf = xbc_ref[...]
    both = jnp.concatenate([prev_sc[(c + 1) % 2], cur_bf], axis=0)
    prev_sc[c % 2] = cur_bf
    shifted = jnp.dot(shift_ref[...], both, preferred_element_type=F32)
    conv = cb_ref[...] + cw_ref[SSD_CONV - 1:SSD_CONV, :] * cur_bf.astype(F32)
    for k in range(1, SSD_CONV):
        conv = conv + (cw_ref[SSD_CONV - 1 - k:SSD_CONV - k, :]
                       * shifted[(k - 1) * CHUNK:k * CHUNK])
    xc = conv * _sigmoid(conv)
    xs = xc[:, :D_MODEL]
    n_bc = SSD_GROUPS * SSD_STATE
    bm = xc[:, D_MODEL:D_MODEL + n_bc]
    cm = xc[:, D_MODEL + n_bc:]

    sm = sm_ref[...]
    lane = lax.broadcasted_iota(jnp.int32, (CHUNK, SMALL_W), 1)
    dt = _softplus(sm + dtb_ref[...])
    xf = sm + bf_ref[...]
    logf = jnp.minimum(xf, 0.0) - jnp.log(1.0 + jnp.exp(-jnp.abs(xf)))
    is_ssd = lane < SSD_HEADS
    row = lax.broadcasted_iota(jnp.int32, (CHUNK, CHUNK), 0)
    col = lax.broadcasted_iota(jnp.int32, (CHUNK, CHUNK), 1)
    causal = row >= col
    tri = jnp.where(causal, 1.0, 0.0).astype(F32)
    cs = jnp.dot(tri, jnp.where(is_ssd, dt * (-jnp.exp(alog_ref[...])), logf),
                 preferred_element_type=F32, precision=lax.Precision.HIGHEST)

    fcum = cs + fcarry_sc[...]
    fcarry_sc[...] = jnp.where(is_ssd[:1], 0.0, fcum[CHUNK - 1:CHUNK, :])
    c2 = fcum * LOG2E
    hi = _bf16_part(c2)
    mid = _bf16_part(c2 - hi)
    lo = (c2 - hi) - mid
    pieces = jnp.concatenate([hi, mid, lo], axis=1).astype(BF16)
    aug = jnp.dot(pieces, sel_ref[...], preferred_element_type=F32) + ones_ref[...]
    ak_ref[...] = aug[:, :SMALL_W].astype(BF16)
    aq_ref[...] = aug[:, SMALL_W:].astype(BF16)

    acs = cs
    alast = acs[CHUNK - 1:CHUNK, :]
    acs_t = acs.T
    per_head = jnp.concatenate([dt, jnp.exp(acs), jnp.exp(alast - acs) * dt], axis=0)
    expd = jnp.dot(per_head.astype(BF16), exp_ref[...], preferred_element_type=F32)
    dt_x, eacs_x, w_x = expd[:CHUNK], expd[CHUNK:2 * CHUNK], expd[2 * CHUNK:]
    dec_x = jnp.dot(jnp.broadcast_to(jnp.exp(alast), (8, SMALL_W)), exp32_ref[...],
                    preferred_element_type=F32, precision=lax.Precision.HIGHEST)[:1]
    xdt = (xs * dt_x).astype(BF16)
    xw = (xs * w_x).astype(BF16)
    state = state_sc[...]
    state_bf = state.astype(BF16)

    half = lax.broadcasted_iota(jnp.int32, (CHUNK, 2 * SSD_HEAD_DIM), 1) < SSD_HEAD_DIM
    hpg = SSD_HEADS // SSD_GROUPS
    gw = hpg * SSD_HEAD_DIM
    y_diag, y_off, s_new = [], [], []
    for g in range(SSD_GROUPS):
        bg = bm[:, g * SSD_STATE:(g + 1) * SSD_STATE]
        cg = cm[:, g * SSD_STATE:(g + 1) * SSD_STATE].astype(BF16)
        cb = lax.dot_general(cg, bg.astype(BF16), (((1,), (1,)), ((), ())),
                             preferred_element_type=F32)
        y_off.append(jnp.dot(cg, state_bf[:, g * gw:(g + 1) * gw], preferred_element_type=F32))
        s_new.append(jnp.dot(bg.T.astype(BF16), xw[:, g * gw:(g + 1) * gw],
                             preferred_element_type=F32))
        for pair in range(hpg // 2):
            gm = []
            for h in (g * hpg + 2 * pair, g * hpg + 2 * pair + 1):
                lmat = jnp.where(causal, jnp.exp(acs[:, h:h + 1] - acs_t[h:h + 1, :]), 0.0)
                gm.append((cb * lmat).astype(BF16))
            k0 = (g * hpg + 2 * pair) * SSD_HEAD_DIM
            xp = xdt[:, k0:k0 + 2 * SSD_HEAD_DIM]
            zero = jnp.zeros_like(xp)
            rhs = jnp.concatenate([jnp.where(half, xp, zero), jnp.where(half, zero, xp)], axis=0)
            y_diag.append(jnp.dot(jnp.concatenate(gm, axis=1), rhs, preferred_element_type=F32))

    state_sc[...] = state * dec_x + jnp.concatenate(s_new, axis=1)
    y = (jnp.concatenate(y_diag, axis=1) + jnp.concatenate(y_off, axis=1) * eacs_x
         + xs * dsk_ref[...])
    zf = z_ref[...].astype(F32)
    y = y * (zf * _sigmoid(zf))
    o_ref[...] = _rms(y, nw_ref[...]).astype(o_ref.dtype)


def _ssd(z, xbc, small, cw, cb, dtb, alog, bf, dsk, nw, *, batch):
    lp = z.shape[1]
    seq_blk = lambda w: pl.BlockSpec((None, CHUNK, w), lambda b, c: (b, c, 0))
    consts = _ssd_consts() + _aug_select()
    return pl.pallas_call(
        _ssd_kernel,
        grid=(batch, lp // CHUNK),
        in_specs=[
            seq_blk(D_MODEL), seq_blk(SSD_CONV_DIM), seq_blk(SMALL_W),
            _resident((SSD_CONV, SSD_CONV_DIM)), _resident((1, SSD_CONV_DIM)),
            _resident((1, SMALL_W)), _resident((1, SMALL_W)), _resident((1, SMALL_W)),
            _resident((1, D_MODEL)), _resident((1, D_MODEL)),
        ] + [_resident(a.shape) for a in consts],
        out_specs=[seq_blk(D_MODEL), seq_blk(SMALL_W), seq_blk(SMALL_W)],
        out_shape=[jax.ShapeDtypeStruct((batch, lp, D_MODEL), BF16),
                   jax.ShapeDtypeStruct((batch, lp, SMALL_W), BF16),
                   jax.ShapeDtypeStruct((batch, lp, SMALL_W), BF16)],
        scratch_shapes=[
            pltpu.VMEM((2, CHUNK, SSD_CONV_DIM), BF16),
            pltpu.VMEM((SSD_STATE, D_MODEL), F32),
            pltpu.VMEM((1, SMALL_W), F32),
        ],
        compiler_params=_cparams(("parallel", "arbitrary")),
        name="ssd",
    )(z, xbc, small, cw, cb, dtb, alog, bf, dsk, nw, *consts)


AUG_LANES = 6
LOG2E = 1.4426950408889634


def _aug_select():
    sel = [[0.0] * (2 * SMALL_W) for _ in range(3 * SMALL_W)]
    ones = [0.0] * (2 * SMALL_W)
    for h in range(FOX_HEADS):
        for p in range(3):
            sel[p * SMALL_W + F_LANE0 + h][AUG_LANES * h + p] = -1.0
            sel[p * SMALL_W + F_LANE0 + h][SMALL_W + AUG_LANES * h + 3 + p] = 1.0
            ones[AUG_LANES * h + 3 + p] = 1.0
            ones[SMALL_W + AUG_LANES * h + p] = 1.0
    return jnp.array(sel, BF16), jnp.array([ones], F32)


def _attn_kernel(q_ref, k_ref, v_ref, ak_ref, aq_ref, o_ref, vt_sc, m_sc, l_sc, acc_sc, s_sc,
                 *, lp):
    tq = ATT_TILE
    hd = FOX_HEAD_DIM
    i = pl.program_id(1)
    n_full = lp // tq
    tail = lp - n_full * tq
    heads = range(FOX_HEADS)

    @pl.when(i == 0)
    def _():
        def tr(c, _):
            r = pl.multiple_of(c * CHUNK, CHUNK)
            for h in heads:
                vt_sc[h, :, pl.ds(r, CHUNK)] = (
                    v_ref[pl.ds(r, CHUNK), h * hd:(h + 1) * hd].astype(F32).T.astype(BF16))
            return 0
        lax.fori_loop(0, lp // CHUNK, tr, 0)

    lane = lax.broadcasted_iota(jnp.int32, (tq, SMALL_W), 1)
    aq = aq_ref[...]
    qf = []
    for h in heads:
        mine = (lane >= AUG_LANES * h) & (lane < AUG_LANES * (h + 1))
        qf.append(jnp.concatenate([q_ref[:, h * hd:(h + 1) * hd],
                                   jnp.where(mine, aq, jnp.zeros_like(aq))], axis=1))

    def tile(nq, diag_tk):
        for h in heads:
            m_sc[h, :, :nq] = jnp.full((1, nq), NEG, F32)
            l_sc[h, :, :nq] = jnp.zeros((1, nq), F32)
            acc_sc[h, :, :nq] = jnp.zeros((hd, nq), F32)

        def block(ks, tk, masked):
            akb = ak_ref[pl.ds(ks, tk), :]
            if masked:
                row = lax.broadcasted_iota(jnp.int32, (tk, nq), 0)
                col = lax.broadcasted_iota(jnp.int32, (tk, nq), 1)
                keep = row <= col

            def scores(h):
                kb = jnp.concatenate([k_ref[pl.ds(ks, tk), h * hd:(h + 1) * hd], akb], axis=1)
                s_sc[h % ATT_AHEAD, :tk, :nq] = lax.dot_general(
                    kb, qf[h][:nq], (((1,), (1,)), ((), ())), preferred_element_type=F32)

            for h in range(ATT_AHEAD - 1):
                scores(h)
            for h in heads:
                if h + ATT_AHEAD - 1 < FOX_HEADS:
                    scores(h + ATT_AHEAD - 1)
                s = s_sc[h % ATT_AHEAD, :tk, :nq]
                if masked:
                    s = jnp.where(keep, s, NEG)
                m = m_sc[h, :, :nq]
                m_new = jnp.maximum(m, jnp.max(s, axis=0, keepdims=True))
                p = jnp.exp2(s - m_new)
                alpha = jnp.exp2(m - m_new)
                m_sc[h, :, :nq] = m_new
                l_sc[h, :, :nq] = alpha * l_sc[h, :, :nq] + jnp.sum(p, axis=0, keepdims=True)
                acc_sc[h, :, :nq] = alpha * acc_sc[h, :, :nq] + jnp.dot(
                    vt_sc[h, :, pl.ds(ks, tk)], p.astype(BF16), preferred_element_type=F32)

        def full_block(kj, _):
            block(pl.multiple_of(kj * tq, tq), tq, False)
            return 0

        lax.fori_loop(0, i, full_block, 0)
        block(pl.multiple_of(i * tq, tq), diag_tk, True)
        for h in heads:
            o_ref[:nq, h * hd:(h + 1) * hd] = (
                acc_sc[h, :, :nq] / l_sc[h, :, :nq]).T.astype(o_ref.dtype)

    if tail:
        pl.when(i < n_full)(lambda: tile(tq, tq))
        pl.when(i == n_full)(lambda: tile(tail, tail))
    else:
        tile(tq, tq)


def _attention(qkv, ak, aq, *, batch):
    lp = qkv.shape[1]
    hd = FOX_HEAD_DIM
    nh = FOX_HEADS
    width = nh * hd
    once = dict(pipeline_mode=pl.Buffered(1))
    return pl.pallas_call(
        functools.partial(_attn_kernel, lp=lp),
        grid=(batch, pl.cdiv(lp, ATT_TILE)),
        in_specs=[
            pl.BlockSpec((None, ATT_TILE, width), lambda b, i: (b, i, 0)),
            pl.BlockSpec((None, lp, width), lambda b, i: (b, 0, 1), **once),
            pl.BlockSpec((None, lp, width), lambda b, i: (b, 0, 2), **once),
            pl.BlockSpec((None, lp, SMALL_W), lambda b, i: (b, 0, 0), **once),
            pl.BlockSpec((None, ATT_TILE, SMALL_W), lambda b, i: (b, i, 0)),
        ],
        out_specs=pl.BlockSpec((None, ATT_TILE, width), lambda b, i: (b, i, 0)),
        out_shape=jax.ShapeDtypeStruct((batch, lp, width), BF16),
        scratch_shapes=[
            pltpu.VMEM((nh, hd, lp), BF16),
            pltpu.VMEM((nh, 1, ATT_TILE), F32),
            pltpu.VMEM((nh, 1, ATT_TILE), F32),
            pltpu.VMEM((nh, hd, ATT_TILE), F32),
            pltpu.VMEM((ATT_AHEAD, ATT_TILE, ATT_TILE), F32),
        ],
        compiler_params=_cparams(("parallel", "arbitrary")),
        name="fox_attn",
    )(qkv, qkv, qkv, ak, aq)


def _s5_kernel(u_ref, perm_ref, perm_t_ref, bblk_ref, are_ref, aim_ref, cblk_ref, dsk_ref,
               wglu_ref, o_ref, h_sc, carry_sc, y_sc, *, batch):
    blk = 2 * S5_HALF

    @pl.when(pl.program_id(0) == 0)
    def _():
        carry_sc[...] = jnp.zeros_like(carry_sc)

    u_bt = u_ref[...].reshape(batch * S5_STEPS, D_MODEL)
    u = jnp.dot(perm_ref[...], u_bt, preferred_element_type=F32).astype(BF16)
    for j in range(S5_BLOCKS):
        h_sc[:, j * blk:(j + 1) * blk] = jnp.dot(
            u[:, j * 128:(j + 1) * 128], bblk_ref[j], preferred_element_type=F32)

    for j in range(S5_BLOCKS):
        re_l = slice(j * blk, j * blk + S5_HALF)
        im_l = slice(j * blk + S5_HALF, (j + 1) * blk)
        a_re = are_ref[:, j * S5_HALF:(j + 1) * S5_HALF]
        a_im = aim_ref[:, j * S5_HALF:(j + 1) * S5_HALF]
        h_re = carry_sc[:, re_l]
        h_im = carry_sc[:, im_l]
        for t in range(S5_STEPS):
            rows_t = slice(t * batch, (t + 1) * batch)
            n_re = a_re * h_re - a_im * h_im + h_sc[rows_t, re_l]
            n_im = a_re * h_im + a_im * h_re + h_sc[rows_t, im_l]
            h_sc[rows_t, re_l] = n_re
            h_sc[rows_t, im_l] = n_im
            h_re, h_im = n_re, n_im
        carry_sc[:, re_l] = h_re
        carry_sc[:, im_l] = h_im

    for j in range(S5_BLOCKS):
        y_sc[:, j * 128:(j + 1) * 128] = jnp.dot(
            h_sc[:, j * blk:(j + 1) * blk].astype(BF16), cblk_ref[j], preferred_element_type=F32)

    y = y_sc[...] + dsk_ref[...] * u.astype(F32)
    y = 0.5 * y * (1.0 + jnp.tanh(0.7978845608028654 * (y + 0.044715 * (y * y * y))))
    gl = jnp.dot(y.astype(BF16), wglu_ref[...], preferred_element_type=F32)
    out_tb = (y * _sigmoid(gl)).astype(BF16)
    out_bt = jnp.dot(perm_t_ref[...], out_tb, preferred_element_type=F32).astype(o_ref.dtype)
    o_ref[...] = out_bt.reshape(batch, S5_STEPS, D_MODEL)


def _s5(u, bblk, a_re, a_im, cblk, dsk, wglu):
    batch, lp, _ = u.shape
    rows = S5_STEPS * batch
    n_state = S5_BLOCKS * 2 * S5_HALF
    tblk = pl.BlockSpec((batch, S5_STEPS, D_MODEL), lambda i: (0, i, 0))
    src = (jnp.arange(rows) % batch) * S5_STEPS + jnp.arange(rows) // batch
    perm = (src[:, None] == jnp.arange(rows)[None, :]).astype(BF16)
    return pl.pallas_call(
        functools.partial(_s5_kernel, batch=batch),
        grid=(lp // S5_STEPS,),
        in_specs=[
            tblk,
            _resident((rows, rows)), _resident((rows, rows)),
            _resident((S5_BLOCKS, 128, 2 * S5_HALF)),
            _resident((1, S5_BLOCKS * S5_HALF)), _resident((1, S5_BLOCKS * S5_HALF)),
            _resident((S5_BLOCKS, 2 * S5_HALF, 128)),
            _resident((1, D_MODEL)),
            _resident((D_MODEL, D_MODEL)),
        ],
        out_specs=tblk,
        out_shape=jax.ShapeDtypeStruct((batch, lp, D_MODEL), BF16),
        scratch_shapes=[
            pltpu.VMEM((rows, n_state), F32),
            pltpu.VMEM((batch, n_state), F32),
            pltpu.VMEM((rows, D_MODEL), F32),
        ],
        compiler_params=_cparams(("arbitrary",)),
        name="s5",
    )(u, perm, perm.T, bblk, a_re, a_im, cblk, dsk, wglu)


def _merge_kernel(ya_ref, yb_ref, yc_ref, gate_ref, x_ref, wb_ref, wo_ref, o_ref):
    acc = None
    for n, y_ref in enumerate((ya_ref, yb_ref, yc_ref)):
        br = jnp.dot(y_ref[...], wb_ref[n], preferred_element_type=F32)
        gt = _sigmoid(gate_ref[:, n * D_MODEL:(n + 1) * D_MODEL].astype(F32))
        acc = gt * br if acc is None else acc + gt * br
    o_ref[...] = x_ref[...] + jnp.dot(acc.astype(BF16), wo_ref[...], preferred_element_type=F32)


def _merge(ya, yb, yc, gate, x, wb, wo):
    rows = x.shape[0]
    rblk = lambda w: pl.BlockSpec((ROW_TILE, w), lambda i: (i, 0))
    return pl.pallas_call(
        _merge_kernel,
        grid=(rows // ROW_TILE,),
        in_specs=[rblk(D_MODEL), rblk(D_MODEL), rblk(D_MODEL), rblk(3 * D_MODEL), rblk(D_MODEL),
                  _resident((3, D_MODEL, D_MODEL)), _resident((D_MODEL, D_MODEL))],
        out_specs=rblk(D_MODEL),
        out_shape=jax.ShapeDtypeStruct((rows, D_MODEL), F32),
        compiler_params=_cparams(("parallel",)),
        name="merge",
    )(ya, yb, yc, gate, x, wb, wo)


def _ffn_kernel(x_ref, nw_ref, w1_ref, w2_ref, o_ref):
    x = x_ref[...]
    xn = _rms(x, nw_ref[...]).astype(BF16)
    half = D_FF // 2
    acc = x
    for c0 in (0, half):
        gp = jnp.dot(xn, w1_ref[:, c0:c0 + half], preferred_element_type=F32)
        up = jnp.dot(xn, w1_ref[:, D_FF + c0:D_FF + c0 + half], preferred_element_type=F32)
        act = (gp * _sigmoid(gp) * up).astype(BF16)
        acc = acc + jnp.dot(act, w2_ref[c0:c0 + half, :], preferred_element_type=F32)
    o_ref[...] = acc


def _ffn(x, nw, w1, w2):
    rows = x.shape[0]
    return pl.pallas_call(
        _ffn_kernel,
        grid=(rows // ROW_TILE,),
        in_specs=[pl.BlockSpec((ROW_TILE, D_MODEL), lambda i: (i, 0)), _resident((1, D_MODEL)),
                  _resident((D_MODEL, 2 * D_FF)), _resident((D_FF, D_MODEL))],
        out_specs=pl.BlockSpec((ROW_TILE, D_MODEL), lambda i: (i, 0)),
        out_shape=jax.ShapeDtypeStruct((rows, D_MODEL), F32),
        compiler_params=_cparams(("parallel",)),
        name="ffn",
    )(x, nw, w1, w2)


def _final_norm_kernel(x_ref, nw_ref, o_ref):
    o_ref[...] = _rms(x_ref[...], nw_ref[...])


def _final_norm(x, nw):
    rows = x.shape[0]
    return pl.pallas_call(
        _final_norm_kernel,
        grid=(rows // ROW_TILE,),
        in_specs=[pl.BlockSpec((ROW_TILE, D_MODEL), lambda i: (i, 0)), _resident((1, D_MODEL))],
        out_specs=pl.BlockSpec((ROW_TILE, D_MODEL), lambda i: (i, 0)),
        out_shape=jax.ShapeDtypeStruct((rows, D_MODEL), F32),
        compiler_params=_cparams(("parallel",)),
        name="final_norm",
    )(x, nw)


def _pad_lanes(v, lane0=0):
    depth, n = v.shape
    out = jnp.zeros((depth, 1, SMALL_W), F32)
    return out.at[:, 0, lane0:lane0 + n].set(v.astype(F32))


def _prep_w_in(w_in):
    o = [0, D_MODEL, D_MODEL + SSD_CONV_DIM]
    o.append(o[-1] + SSD_HEADS)
    o.append(o[-1] + 3 * D_MODEL)
    o.append(o[-1] + FOX_HEADS)
    o.append(o[-1] + D_MODEL)
    o.append(o[-1] + 3 * D_MODEL)
    z, xbc, dt, qkv, fr, u, gate = [w_in[..., o[k]:o[k + 1]] for k in range(7)]
    pad = jnp.zeros(w_in.shape[:-1] + (SMALL_W - SSD_HEADS - FOX_HEADS,), w_in.dtype)
    q_scale = jnp.concatenate([jnp.full((D_MODEL,), FOX_HEAD_DIM ** -0.5 * LOG2E, w_in.dtype),
                               jnp.ones((2 * D_MODEL,), w_in.dtype)])
    return jnp.concatenate([z, xbc, qkv * q_scale, u, gate, dt, fr, pad], axis=-1).astype(BF16)


def _prep_s5(lam_re, lam_im, b_re, b_im, c_re, c_im, log_step):
    depth = lam_re.shape[0]
    lam = lax.complex(lam_re.astype(F32), lam_im.astype(F32))
    step = jnp.exp(log_step.astype(F32))[..., None]
    lam_bar = jnp.exp(lam * step)
    b_bar = ((lam_bar - 1.0) / lam)[..., None] * lax.complex(b_re.astype(F32), b_im.astype(F32))
    nb, gb, p, ch = S5_BLOCKS, S5_BLOCK_GROUPS, S5_STATE, S5_GROUP
    eye = jnp.eye(gb, dtype=F32)

    def b_block(part):
        part = part.reshape(depth, nb, gb, p, ch)
        m = jnp.einsum('djgpc,gh->djgchp', part, eye)
        return m.reshape(depth, nb, gb * ch, gb * p)

    bblk = jnp.concatenate([b_block(jnp.real(b_bar)), b_block(jnp.imag(b_bar))], axis=-1)

    def c_block(part):
        part = part.reshape(depth, nb, gb, ch, p)
        m = jnp.einsum('djgcp,gh->djgphc', part, eye)
        return m.reshape(depth, nb, gb * p, gb * ch)

    cblk = jnp.concatenate([c_block(c_re.astype(F32)), c_block(-c_im.astype(F32))], axis=-2)
    a_re = jnp.real(lam_bar).reshape(depth, 1, nb * gb * p)
    a_im = jnp.imag(lam_bar).reshape(depth, 1, nb * gb * p)
    return bblk.astype(BF16), a_re, a_im, cblk.astype(BF16)


def kernel(x, meta, norm1, w_in, ssd_conv_w, ssd_conv_b, ssd_dt_bias, ssd_a_log, ssd_d, ssd_norm,
           fox_bf, s5_lam_re, s5_lam_im, s5_b_re, s5_b_im, s5_c_re, s5_c_im, s5_log_step, s5_d,
           s5_w_glu, w_branch, w_out, norm2, w_ffn_in, w_ffn_out, norm_f):
    batch, seq, _ = x.shape
    length = seq + N_META
    lp = -(-length // CHUNK) * CHUNK
    assert batch % 8 == 0 and (lp * batch) % ROW_TILE == 0 and lp % S5_STEPS == 0
    rows = lp * batch

    xt = jnp.concatenate([
        jnp.broadcast_to(meta[None].astype(F32), (batch, N_META, D_MODEL)),
        x.astype(F32),
        jnp.zeros((batch, lp - length, D_MODEL), F32)], axis=1).reshape(rows, D_MODEL)

    bblk, a_re, a_im, cblk = _prep_s5(s5_lam_re, s5_lam_im, s5_b_re, s5_b_im, s5_c_re, s5_c_im,
                                      s5_log_step)
    row3 = lambda v: v.astype(F32)[:, None, :]
    layers = dict(
        norm1=row3(norm1), w_in=_prep_w_in(w_in),
        conv_w=ssd_conv_w.astype(F32), conv_b=row3(ssd_conv_b),
        dt_bias=_pad_lanes(ssd_dt_bias), a_log=_pad_lanes(ssd_a_log),
        ssd_d=row3(jnp.repeat(ssd_d, SSD_HEAD_DIM, axis=-1)), ssd_norm=row3(ssd_norm),
        fox_bf=_pad_lanes(fox_bf, F_LANE0),
        bblk=bblk, a_re=a_re, a_im=a_im, cblk=cblk, s5_d=row3(s5_d), w_glu=s5_w_glu.astype(BF16),
        w_branch=w_branch.astype(BF16), w_out=w_out.astype(BF16), norm2=row3(norm2),
        w_ffn_in=w_ffn_in.astype(BF16), w_ffn_out=w_ffn_out.astype(BF16),
    )

    def layer(xr, p):
        z, xbc, qkv, u, gate, small = _inproj(xr, p['norm1'], p['w_in'])
        seqv = lambda a: a.reshape(batch, lp, a.shape[1])
        small_s = seqv(small)
        y_a, ak, aq = _ssd(seqv(z), seqv(xbc), small_s, p['conv_w'], p['conv_b'], p['dt_bias'],
                           p['a_log'], p['fox_bf'], p['ssd_d'], p['ssd_norm'], batch=batch)
        y_a = y_a.reshape(rows, D_MODEL)
        y_b = _attention(seqv(qkv), ak, aq, batch=batch).reshape(rows, D_MODEL)
        y_c = _s5(seqv(u), p['bblk'], p['a_re'], p['a_im'], p['cblk'], p['s5_d'],
                  p['w_glu']).reshape(rows, D_MODEL)
        x1 = _merge(y_a, y_b, y_c, gate, xr, p['w_branch'], p['w_out'])
        return _ffn(x1, p['norm2'], p['w_ffn_in'], p['w_ffn_out']), None

    xr, _ = lax.scan(layer, xt, layers)
    out = _final_norm(xr, norm_f.astype(F32)[None, :]).reshape(batch, lp, D_MODEL)
    return out[:, N_META:length]
```

```python
import functools

import jax
import jax.numpy as jnp
from jax import lax
from jax.experimental import pallas as pl
from jax.experimental.pallas import tpu as pltpu

F32 = jnp.float32
BF16 = jnp.bfloat16

D_MODEL = 1024
N_META = 16
CHUNK = 128
EPS = 1e-6
NEG = -1e30

SSD_HEADS = 16
SSD_HEAD_DIM = 64
SSD_GROUPS = 2
SSD_STATE = 128
SSD_CONV = 4
SSD_CONV_DIM = D_MODEL + 2 * SSD_GROUPS * SSD_STATE

FOX_HEADS = 8
FOX_HEAD_DIM = 128

S5_GROUP = 16
S5_GROUPS = D_MODEL // S5_GROUP
S5_STATE = 64
S5_BLOCKS = 8
S5_BLOCK_GROUPS = S5_GROUPS // S5_BLOCKS
S5_HALF = S5_BLOCK_GROUPS * S5_STATE

D_FF = 2816
SMALL_W = 128
F_LANE0 = SSD_HEADS

SEC_WIDTHS = (D_MODEL, SSD_CONV_DIM, 3 * D_MODEL, D_MODEL, 3 * D_MODEL, SMALL_W)
D_IN_P = sum(SEC_WIDTHS)

ROW_TILE = 512
COL_CHUNK = 512
ATT_TILE = 512
ATT_AHEAD = 3
ATT_KEY_TILES = 2
S5_STEPS = 16
VMEM_LIMIT = 56 * 1024 * 1024


def _cparams(sems):
    return pltpu.CompilerParams(dimension_semantics=sems, vmem_limit_bytes=VMEM_LIMIT)


def _resident(shape):
    nd = len(shape)
    return pl.BlockSpec(shape, lambda *_: (0,) * nd, pipeline_mode=pl.Buffered(1))


def _sigmoid(x):
    return 1.0 / (1.0 + jnp.exp(-x))


def _softplus(x):
    return jnp.maximum(x, 0.0) + jnp.log(1.0 + jnp.exp(-jnp.abs(x)))


def _bf16_part(x):
    bits = lax.bitcast_convert_type(x, jnp.uint32) & jnp.uint32(0xFFFF0000)
    return lax.bitcast_convert_type(bits, F32)


def _rms(x, w):
    ms = jnp.mean(x * x, axis=-1, keepdims=True)
    return (x * lax.rsqrt(ms + EPS)) * w


def _inproj_kernel(x_ref, nw_ref, w_ref, *out_refs):
    xn = _rms(x_ref[...], nw_ref[...]).astype(BF16)
    off = 0
    for ref in out_refs:
        width = ref.shape[1]
        for c0 in range(0, width, COL_CHUNK):
            cw = min(COL_CHUNK, width - c0)
            ref[:, c0:c0 + cw] = jnp.dot(
                xn, w_ref[:, off + c0:off + c0 + cw], preferred_element_type=F32
            ).astype(ref.dtype)
        off += width


def _inproj(x, nw, w):
    rows = x.shape[0]
    dtypes = (BF16, BF16, BF16, BF16, BF16, F32)
    return pl.pallas_call(
        _inproj_kernel,
        grid=(rows // ROW_TILE,),
        in_specs=[
            pl.BlockSpec((ROW_TILE, D_MODEL), lambda i: (i, 0)),
            _resident((1, D_MODEL)),
            _resident((D_MODEL, D_IN_P)),
        ],
        out_specs=[pl.BlockSpec((ROW_TILE, wd), lambda i: (i, 0)) for wd in SEC_WIDTHS],
        out_shape=[jax.ShapeDtypeStruct((rows, wd), dt) for wd, dt in zip(SEC_WIDTHS, dtypes)],
        compiler_params=_cparams(("parallel",)),
        name="inproj",
    )(x, nw, w)


def _ssd_consts():
    t = jnp.arange(CHUNK)
    cols = jnp.arange(2 * CHUNK)
    shift = jnp.concatenate(
        [(cols[None, :] == (CHUNK + t - k)[:, None]) for k in range(1, SSD_CONV)], axis=0)
    head_of_lane = jnp.arange(D_MODEL) // SSD_HEAD_DIM
    expand = jnp.arange(SMALL_W)[:, None] == head_of_lane[None, :]
    return shift.astype(BF16), expand.astype(BF16), expand.astype(F32)


def _ssd_kernel(z_ref, xbc_ref, sm_ref, cw_ref, cb_ref, dtb_ref, alog_ref, bf_ref, dsk_ref, nw_ref,
                shift_ref, exp_ref, exp32_ref, sel_ref, ones_ref,
                o_ref, ak_ref, aq_ref, prev_sc, state_sc, fcarry_sc):
    c = pl.program_id(1)

    @pl.when(c == 0)
    def _():
        prev_sc[...] = jnp.zeros_like(prev_sc)
        state_sc[...] = jnp.zeros_like(state_sc)
        fcarry_sc[...] = jnp.zeros_like(fcarry_sc)

    cur_bf = xbc_ref[...]
    both = jnp.concatenate([prev_sc[(c + 1) % 2], cur_bf], axis=0)
    prev_sc[c % 2] = cur_bf
    shifted = jnp.dot(shift_ref[...], both, preferred_element_type=F32)
    conv = cb_ref[...] + cw_ref[SSD_CONV - 1:SSD_CONV, :] * cur_bf.astype(F32)
    for k in range(1, SSD_CONV):
        conv = conv + (cw_ref[SSD_CONV - 1 - k:SSD_CONV - k, :]
                       * shifted[(k - 1) * CHUNK:k * CHUNK])
    xc = conv * _sigmoid(conv)
    xs = xc[:, :D_MODEL]
    n_bc = SSD_GROUPS * SSD_STATE
    bm = xc[:, D_MODEL:D_MODEL + n_bc]
    cm = xc[:, D_MODEL + n_bc:]

    sm = sm_ref[...]
    lane = lax.broadcasted_iota(jnp.int32, (CHUNK, SMALL_W), 1)
    dt = _softplus(sm + dtb_ref[...])
    xf = sm + bf_ref[...]
    logf = jnp.minimum(xf, 0.0) - jnp.log(1.0 + jnp.exp(-jnp.abs(xf)))
    is_ssd = lane < SSD_HEADS
    row = lax.broadcasted_iota(jnp.int32, (CHUNK, CHUNK), 0)
    col = lax.broadcasted_iota(jnp.int32, (CHUNK, CHUNK), 1)
    causal = row >= col
    tri = jnp.where(causal, 1.0, 0.0).astype(F32)
    cs = jnp.dot(tri, jnp.where(is_ssd, dt * (-jnp.exp(alog_ref[...])), logf),
                 preferred_element_type=F32, precision=lax.Precision.HIGHEST)

    fcum = cs + fcarry_sc[...]
    fcarry_sc[...] = jnp.where(is_ssd[:1], 0.0, fcum[CHUNK - 1:CHUNK, :])
    c2 = fcum * LOG2E
    hi = _bf16_part(c2)
    mid = _bf16_part(c2 - hi)
    lo = (c2 - hi) - mid
    pieces = jnp.concatenate([hi, mid, lo], axis=1).astype(BF16)
    aug = jnp.dot(pieces, sel_ref[...], preferred_element_type=F32) + ones_ref[...]
    ak_ref[...] = aug[:, :SMALL_W].astype(BF16)
    aq_ref[...] = aug[:, SMALL_W:].astype(BF16)

    acs = cs
    alast = acs[CHUNK - 1:CHUNK, :]
    acs_t = acs.T
    per_head = jnp.concatenate([dt, jnp.exp(acs), jnp.exp(alast - acs) * dt], axis=0)
    expd = jnp.dot(per_head.astype(BF16), exp_ref[...], preferred_element_type=F32)
    dt_x, eacs_x, w_x = expd[:CHUNK], expd[CHUNK:2 * CHUNK], expd[2 * CHUNK:]
    dec_x = jnp.dot(jnp.broadcast_to(jnp.exp(alast), (8, SMALL_W)), exp32_ref[...],
                    preferred_element_type=F32, precision=lax.Precision.HIGHEST)[:1]
    xdt = (xs * dt_x).astype(BF16)
    xw = (xs * w_x).astype(BF16)
    state = state_sc[...]
    state_bf = state.astype(BF16)

    half = lax.broadcasted_iota(jnp.int32, (CHUNK, 2 * SSD_HEAD_DIM), 1) < SSD_HEAD_DIM
    hpg = SSD_HEADS // SSD_GROUPS
    gw = hpg * SSD_HEAD_DIM
    y_diag, y_off, s_new = [], [], []
    for g in range(SSD_GROUPS):
        bg = bm[:, g * SSD_STATE:(g + 1) * SSD_STATE]
        cg = cm[:, g * SSD_STATE:(g + 1) * SSD_STATE].astype(BF16)
        cb = lax.dot_general(cg, bg.astype(BF16), (((1,), (1,)), ((), ())),
                             preferred_element_type=F32)
        y_off.append(jnp.dot(cg, state_bf[:, g * gw:(g + 1) * gw], preferred_element_type=F32))
        s_new.append(jnp.dot(bg.T.astype(BF16), xw[:, g * gw:(g + 1) * gw],
                             preferred_element_type=F32))
        for pair in range(hpg // 2):
            gm = []
            for h in (g * hpg + 2 * pair, g * hpg + 2 * pair + 1):
                lmat = jnp.where(causal, jnp.exp(acs[:, h:h + 1] - acs_t[h:h + 1, :]), 0.0)
                gm.append((cb * lmat).astype(BF16))
            k0 = (g * hpg + 2 * pair) * SSD_HEAD_DIM
            xp = xdt[:, k0:k0 + 2 * SSD_HEAD_DIM]
            zero = jnp.zeros_like(xp)
            rhs = jnp.concatenate([jnp.where(half, xp, zero), jnp.where(half, zero, xp)], axis=0)
            y_diag.append(jnp.dot(jnp.concatenate(gm, axis=1), rhs, preferred_element_type=F32))

    state_sc[...] = state * dec_x + jnp.concatenate(s_new, axis=1)
    y = (jnp.concatenate(y_diag, axis=1) + jnp.concatenate(y_off, axis=1) * eacs_x
         + xs * dsk_ref[...])
    zf = z_ref[...].astype(F32)
    y = y * (zf * _sigmoid(zf))
    o_ref[...] = _rms(y, nw_ref[...]).astype(o_ref.dtype)


def _ssd(z, xbc, small, cw, cb, dtb, alog, bf, dsk, nw, *, batch):
    lp = z.shape[1]
    seq_blk = lambda w: pl.BlockSpec((None, CHUNK, w), lambda b, c: (b, c, 0))
    consts = _ssd_consts() + _aug_select()
    return pl.pallas_call(
        _ssd_kernel,
        grid=(batch, lp // CHUNK),
        in_specs=[
            seq_blk(D_MODEL), seq_blk(SSD_CONV_DIM), seq_blk(SMALL_W),
            _resident((SSD_CONV, SSD_CONV_DIM)), _resident((1, SSD_CONV_DIM)),
            _resident((1, SMALL_W)), _resident((1, SMALL_W)), _resident((1, SMALL_W)),
            _resident((1, D_MODEL)), _resident((1, D_MODEL)),
        ] + [_resident(a.shape) for a in consts],
        out_specs=[seq_blk(D_MODEL), seq_blk(SMALL_W), seq_blk(SMALL_W)],
        out_shape=[jax.ShapeDtypeStruct((batch, lp, D_MODEL), BF16),
                   jax.ShapeDtypeStruct((batch, lp, SMALL_W), BF16),
                   jax.ShapeDtypeStruct((batch, lp, SMALL_W), BF16)],
        scratch_shapes=[
            pltpu.VMEM((2, CHUNK, SSD_CONV_DIM), BF16),
            pltpu.VMEM((SSD_STATE, D_MODEL), F32),
            pltpu.VMEM((1, SMALL_W), F32),
        ],
        compiler_params=_cparams(("parallel", "arbitrary")),
        name="ssd",
    )(z, xbc, small, cw, cb, dtb, alog, bf, dsk, nw, *consts)


AUG_LANES = 6
LOG2E = 1.4426950408889634


def _aug_select():
    sel = [[0.0] * (2 * SMALL_W) for _ in range(3 * SMALL_W)]
    ones = [0.0] * (2 * SMALL_W)
    for h in range(FOX_HEADS):
        for p in range(3):
            sel[p * SMALL_W + F_LANE0 + h][AUG_LANES * h + p] = -1.0
            sel[p * SMALL_W + F_LANE0 + h][SMALL_W + AUG_LANES * h + 3 + p] = 1.0
            ones[AUG_LANES * h + 3 + p] = 1.0
            ones[SMALL_W + AUG_LANES * h + p] = 1.0
    return jnp.array(sel, BF16), jnp.array([ones], F32)


def _attn_kernel(q_ref, k_ref, v_ref, ak_ref, aq_ref, o_ref, vt_sc, m_sc, l_sc, acc_sc, s_sc,
                 *, lp):
    tq = ATT_TILE
    hd = FOX_HEAD_DIM
    i = pl.program_id(1)
    n_full = lp // tq
    tail = lp - n_full * tq
    heads = range(FOX_HEADS)

    @pl.when(i == 0)
    def _():
        def tr(c, _):
            r = pl.multiple_of(c * CHUNK, CHUNK)
            for h in heads:
                vt_sc[h, :, pl.ds(r, CHUNK)] = (
                    v_ref[pl.ds(r, CHUNK), h * hd:(h + 1) * hd].astype(F32).T.astype(BF16))
            return 0
        lax.fori_loop(0, lp // CHUNK, tr, 0)

    lane = lax.broadcasted_iota(jnp.int32, (tq, SMALL_W), 1)
    aq = aq_ref[...]
    qf = []
    for h in heads:
        mine = (lane >= AUG_LANES * h) & (lane < AUG_LANES * (h + 1))
        qf.append(jnp.concatenate([q_ref[:, h * hd:(h + 1) * hd],
                                   jnp.where(mine, aq, jnp.zeros_like(aq))], axis=1))

    def tile(nq, diag_tk):
        for h in heads:
            m_sc[h, :, :nq] = jnp.full((1, nq), NEG, F32)
            l_sc[h, :, :nq] = jnp.zeros((1, nq), F32)
            acc_sc[h, :, :nq] = jnp.zeros((hd, nq), F32)

        def block(ks, tk, masked):
            akb = ak_ref[pl.ds(ks, tk), :]
            if masked:
                row = lax.broadcasted_iota(jnp.int32, (tk, nq), 0)
                col = lax.broadcasted_iota(jnp.int32, (tk, nq), 1)
                keep = row <= col

            def scores(h):
                kb = jnp.concatenate([k_ref[pl.ds(ks, tk), h * hd:(h + 1) * hd], akb], axis=1)
                s_sc[h % ATT_AHEAD, :tk, :nq] = lax.dot_general(
                    kb, qf[h][:nq], (((1,), (1,)), ((), ())), preferred_element_type=F32)

            for h in range(ATT_AHEAD - 1):
                scores(h)
            for h in heads:
                if h + ATT_AHEAD - 1 < FOX_HEADS:
                    scores(h + ATT_AHEAD - 1)
                s = s_sc[h % ATT_AHEAD, :tk, :nq]
                if masked:
                    s = jnp.where(keep, s, NEG)
                m = m_sc[h, :, :nq]
                m_new = jnp.maximum(m, jnp.max(s, axis=0, keepdims=True))
                p = jnp.exp2(s - m_new)
                alpha = jnp.exp2(m - m_new)
                m_sc[h, :, :nq] = m_new
                l_sc[h, :, :nq] = alpha * l_sc[h, :, :nq] + jnp.sum(p, axis=0, keepdims=True)
                acc_sc[h, :, :nq] = alpha * acc_sc[h, :, :nq] + jnp.dot(
                    vt_sc[h, :, pl.ds(ks, tk)], p.astype(BF16), preferred_element_type=F32)

        big = ATT_KEY_TILES * tq

        def big_block(kj, _):
            block(pl.multiple_of(kj * big, big), big, False)
            return 0

        def full_block(kj, _):
            block(pl.multiple_of(((i // ATT_KEY_TILES) * ATT_KEY_TILES + kj) * tq, tq), tq, False)
            return 0

        lax.fori_loop(0, i // ATT_KEY_TILES, big_block, 0)
        lax.fori_loop(0, i % ATT_KEY_TILES, full_block, 0)
        block(pl.multiple_of(i * tq, tq), diag_tk, True)
        for h in heads:
            o_ref[:nq, h * hd:(h + 1) * hd] = (
                acc_sc[h, :, :nq] / l_sc[h, :, :nq]).T.astype(o_ref.dtype)

    if tail:
        pl.when(i < n_full)(lambda: tile(tq, tq))
        pl.when(i == n_full)(lambda: tile(tail, tail))
    else:
        tile(tq, tq)


def _attention(qkv, ak, aq, *, batch):
    lp = qkv.shape[1]
    hd = FOX_HEAD_DIM
    nh = FOX_HEADS
    width = nh * hd
    once = dict(pipeline_mode=pl.Buffered(1))
    return pl.pallas_call(
        functools.partial(_attn_kernel, lp=lp),
        grid=(batch, pl.cdiv(lp, ATT_TILE)),
        in_specs=[
            pl.BlockSpec((None, ATT_TILE, width), lambda b, i: (b, i, 0)),
            pl.BlockSpec((None, lp, width), lambda b, i: (b, 0, 1), **once),
            pl.BlockSpec((None, lp, width), lambda b, i: (b, 0, 2), **once),
            pl.BlockSpec((None, lp, SMALL_W), lambda b, i: (b, 0, 0), **once),
            pl.BlockSpec((None, ATT_TILE, SMALL_W), lambda b, i: (b, i, 0)),
        ],
        out_specs=pl.BlockSpec((None, ATT_TILE, width), lambda b, i: (b, i, 0)),
        out_shape=jax.ShapeDtypeStruct((batch, lp, width), BF16),
        scratch_shapes=[
            pltpu.VMEM((nh, hd, lp), BF16),
            pltpu.VMEM((nh, 1, ATT_TILE), F32),
            pltpu.VMEM((nh, 1, ATT_TILE), F32),
            pltpu.VMEM((nh, hd, ATT_TILE), F32),
            pltpu.VMEM((ATT_AHEAD, ATT_KEY_TILES * ATT_TILE, ATT_TILE), F32),
        ],
        compiler_params=_cparams(("parallel", "arbitrary")),
        name="fox_attn",
    )(qkv, qkv, qkv, ak, aq)


def _s5_kernel(u_ref, perm_ref, perm_t_ref, bblk_ref, are_ref, aim_ref, cblk_ref, dsk_ref,
               wglu_ref, o_ref, h_sc, carry_sc, y_sc, *, batch):
    blk = 2 * S5_HALF

    @pl.when(pl.program_id(0) == 0)
    def _():
        carry_sc[...] = jnp.zeros_like(carry_sc)

    u_bt = u_ref[...].reshape(batch * S5_STEPS, D_MODEL)
    u = jnp.dot(perm_ref[...], u_bt, preferred_element_type=F32).astype(BF16)
    for j in range(S5_BLOCKS):
        h_sc[:, j * blk:(j + 1) * blk] = jnp.dot(
            u[:, j * 128:(j + 1) * 128], bblk_ref[j], preferred_element_type=F32)

    for j in range(S5_BLOCKS):
        re_l = slice(j * blk, j * blk + S5_HALF)
        im_l = slice(j * blk + S5_HALF, (j + 1) * blk)
        a_re = are_ref[:, j * S5_HALF:(j + 1) * S5_HALF]
        a_im = aim_ref[:, j * S5_HALF:(j + 1) * S5_HALF]
        h_re = carry_sc[:, re_l]
        h_im = carry_sc[:, im_l]
        for t in range(S5_STEPS):
            rows_t = slice(t * batch, (t + 1) * batch)
            n_re = a_re * h_re - a_im * h_im + h_sc[rows_t, re_l]
            n_im = a_re * h_im + a_im * h_re + h_sc[rows_t, im_l]
            h_sc[rows_t, re_l] = n_re
            h_sc[rows_t, im_l] = n_im
            h_re, h_im = n_re, n_im
        carry_sc[:, re_l] = h_re
        carry_sc[:, im_l] = h_im

    for j in range(S5_BLOCKS):
        y_sc[:, j * 128:(j + 1) * 128] = jnp.dot(
            h_sc[:, j * blk:(j + 1) * blk].astype(BF16), cblk_ref[j], preferred_element_type=F32)

    y = y_sc[...] + dsk_ref[...] * u.astype(F32)
    y = 0.5 * y * (1.0 + jnp.tanh(0.7978845608028654 * (y + 0.044715 * (y * y * y))))
    gl = jnp.dot(y.astype(BF16), wglu_ref[...], preferred_element_type=F32)
    out_tb = (y * _sigmoid(gl)).astype(BF16)
    out_bt = jnp.dot(perm_t_ref[...], out_tb, preferred_element_type=F32).astype(o_ref.dtype)
    o_ref[...] = out_bt.reshape(batch, S5_STEPS, D_MODEL)


def _s5(u, bblk, a_re, a_im, cblk, dsk, wglu):
    batch, lp, _ = u.shape
    rows = S5_STEPS * batch
    n_state = S5_BLOCKS * 2 * S5_HALF
    tblk = pl.BlockSpec((batch, S5_STEPS, D_MODEL), lambda i: (0, i, 0))
    src = (jnp.arange(rows) % batch) * S5_STEPS + jnp.arange(rows) // batch
    perm = (src[:, None] == jnp.arange(rows)[None, :]).astype(BF16)
    return pl.pallas_call(
        functools.partial(_s5_kernel, batch=batch),
        grid=(lp // S5_STEPS,),
        in_specs=[
            tblk,
            _resident((rows, rows)), _resident((rows, rows)),
            _resident((S5_BLOCKS, 128, 2 * S5_HALF)),
            _resident((1, S5_BLOCKS * S5_HALF)), _resident((1, S5_BLOCKS * S5_HALF)),
            _resident((S5_BLOCKS, 2 * S5_HALF, 128)),
            _resident((1, D_MODEL)),
            _resident((D_MODEL, D_MODEL)),
        ],
        out_specs=tblk,
        out_shape=jax.ShapeDtypeStruct((batch, lp, D_MODEL), BF16),
        scratch_shapes=[
            pltpu.VMEM((rows, n_state), F32),
            pltpu.VMEM((batch, n_state), F32),
            pltpu.VMEM((rows, D_MODEL), F32),
        ],
        compiler_params=_cparams(("arbitrary",)),
        name="s5",
    )(u, perm, perm.T, bblk, a_re, a_im, cblk, dsk, wglu)


def _merge_kernel(ya_ref, yb_ref, yc_ref, gate_ref, x_ref, wb_ref, wo_ref, o_ref):
    acc = None
    for n, y_ref in enumerate((ya_ref, yb_ref, yc_ref)):
        br = jnp.dot(y_ref[...], wb_ref[n], preferred_element_type=F32)
        gt = _sigmoid(gate_ref[:, n * D_MODEL:(n + 1) * D_MODEL].astype(F32))
        acc = gt * br if acc is None else acc + gt * br
    o_ref[...] = x_ref[...] + jnp.dot(acc.astype(BF16), wo_ref[...], preferred_element_type=F32)


def _merge(ya, yb, yc, gate, x, wb, wo):
    rows = x.shape[0]
    rblk = lambda w: pl.BlockSpec((ROW_TILE, w), lambda i: (i, 0))
    return pl.pallas_call(
        _merge_kernel,
        grid=(rows // ROW_TILE,),
        in_specs=[rblk(D_MODEL), rblk(D_MODEL), rblk(D_MODEL), rblk(3 * D_MODEL), rblk(D_MODEL),
                  _resident((3, D_MODEL, D_MODEL)), _resident((D_MODEL, D_MODEL))],
        out_specs=rblk(D_MODEL),
        out_shape=jax.ShapeDtypeStruct((rows, D_MODEL), F32),
        compiler_params=_cparams(("parallel",)),
        name="merge",
    )(ya, yb, yc, gate, x, wb, wo)


def _ffn_kernel(x_ref, nw_ref, w1_ref, w2_ref, o_ref):
    x = x_ref[...]
    xn = _rms(x, nw_ref[...]).astype(BF16)
    half = D_FF // 2
    acc = x
    for c0 in (0, half):
        gp = jnp.dot(xn, w1_ref[:, c0:c0 + half], preferred_element_type=F32)
        up = jnp.dot(xn, w1_ref[:, D_FF + c0:D_FF + c0 + half], preferred_element_type=F32)
        act = (gp * _sigmoid(gp) * up).astype(BF16)
        acc = acc + jnp.dot(act, w2_ref[c0:c0 + half, :], preferred_element_type=F32)
    o_ref[...] = acc


def _ffn(x, nw, w1, w2):
    rows = x.shape[0]
    return pl.pallas_call(
        _ffn_kernel,
        grid=(rows // ROW_TILE,),
        in_specs=[pl.BlockSpec((ROW_TILE, D_MODEL), lambda i: (i, 0)), _resident((1, D_MODEL)),
                  _resident((D_MODEL, 2 * D_FF)), _resident((D_FF, D_MODEL))],
        out_specs=pl.BlockSpec((ROW_TILE, D_MODEL), lambda i: (i, 0)),
        out_shape=jax.ShapeDtypeStruct((rows, D_MODEL), F32),
        compiler_params=_cparams(("parallel",)),
        name="ffn",
    )(x, nw, w1, w2)


def _final_norm_kernel(x_ref, nw_ref, o_ref):
    o_ref[...] = _rms(x_ref[...], nw_ref[...])


def _final_norm(x, nw):
    rows = x.shape[0]
    return pl.pallas_call(
        _final_norm_kernel,
        grid=(rows // ROW_TILE,),
        in_specs=[pl.BlockSpec((ROW_TILE, D_MODEL), lambda i: (i, 0)), _resident((1, D_MODEL))],
        out_specs=pl.BlockSpec((ROW_TILE, D_MODEL), lambda i: (i, 0)),
        out_shape=jax.ShapeDtypeStruct((rows, D_MODEL), F32),
        compiler_params=_cparams(("parallel",)),
        name="final_norm",
    )(x, nw)


def _pad_lanes(v, lane0=0):
    depth, n = v.shape
    out = jnp.zeros((depth, 1, SMALL_W), F32)
    return out.at[:, 0, lane0:lane0 + n].set(v.astype(F32))


def _prep_w_in(w_in):
    o = [0, D_MODEL, D_MODEL + SSD_CONV_DIM]
    o.append(o[-1] + SSD_HEADS)
    o.append(o[-1] + 3 * D_MODEL)
    o.append(o[-1] + FOX_HEADS)
    o.append(o[-1] + D_MODEL)
    o.append(o[-1] + 3 * D_MODEL)
    z, xbc, dt, qkv, fr, u, gate = [w_in[..., o[k]:o[k + 1]] for k in range(7)]
    pad = jnp.zeros(w_in.shape[:-1] + (SMALL_W - SSD_HEADS - FOX_HEADS,), w_in.dtype)
    q_scale = jnp.concatenate([jnp.full((D_MODEL,), FOX_HEAD_DIM ** -0.5 * LOG2E, w_in.dtype),
                               jnp.ones((2 * D_MODEL,), w_in.dtype)])
    return jnp.concatenate([z, xbc, qkv * q_scale, u, gate, dt, fr, pad], axis=-1).astype(BF16)


def _prep_s5(lam_re, lam_im, b_re, b_im, c_re, c_im, log_step):
    depth = lam_re.shape[0]
    lam = lax.complex(lam_re.astype(F32), lam_im.astype(F32))
    step = jnp.exp(log_step.astype(F32))[..., None]
    lam_bar = jnp.exp(lam * step)
    b_bar = ((lam_bar - 1.0) / lam)[..., None] * lax.complex(b_re.astype(F32), b_im.astype(F32))
    nb, gb, p, ch = S5_BLOCKS, S5_BLOCK_GROUPS, S5_STATE, S5_GROUP
    eye = jnp.eye(gb, dtype=F32)

    def b_block(part):
        part = part.reshape(depth, nb, gb, p, ch)
        m = jnp.einsum('djgpc,gh->djgchp', part, eye)
        return m.reshape(depth, nb, gb * ch, gb * p)

    bblk = jnp.concatenate([b_block(jnp.real(b_bar)), b_block(jnp.imag(b_bar))], axis=-1)

    def c_block(part):
        part = part.reshape(depth, nb, gb, ch, p)
        m = jnp.einsum('djgcp,gh->djgphc', part, eye)
        return m.reshape(depth, nb, gb * p, gb * ch)

    cblk = jnp.concatenate([c_block(c_re.astype(F32)), c_block(-c_im.astype(F32))], axis=-2)
    a_re = jnp.real(lam_bar).reshape(depth, 1, nb * gb * p)
    a_im = jnp.imag(lam_bar).reshape(depth, 1, nb * gb * p)
    return bblk.astype(BF16), a_re, a_im, cblk.astype(BF16)


def kernel(x, meta, norm1, w_in, ssd_conv_w, ssd_conv_b, ssd_dt_bias, ssd_a_log, ssd_d, ssd_norm,
           fox_bf, s5_lam_re, s5_lam_im, s5_b_re, s5_b_im, s5_c_re, s5_c_im, s5_log_step, s5_d,
           s5_w_glu, w_branch, w_out, norm2, w_ffn_in, w_ffn_out, norm_f):
    batch, seq, _ = x.shape
    length = seq + N_META
    lp = -(-length // CHUNK) * CHUNK
    assert batch % 8 == 0 and (lp * batch) % ROW_TILE == 0 and lp % S5_STEPS == 0
    rows = lp * batch

    xt = jnp.concatenate([
        jnp.broadcast_to(meta[None].astype(F32), (batch, N_META, D_MODEL)),
        x.astype(F32),
        jnp.zeros((batch, lp - length, D_MODEL), F32)], axis=1).reshape(rows, D_MODEL)

    bblk, a_re, a_im, cblk = _prep_s5(s5_lam_re, s5_lam_im, s5_b_re, s5_b_im, s5_c_re, s5_c_im,
                                      s5_log_step)
    row3 = lambda v: v.astype(F32)[:, None, :]
    layers = dict(
        norm1=row3(norm1), w_in=_prep_w_in(w_in),
        conv_w=ssd_conv_w.astype(F32), conv_b=row3(ssd_conv_b),
        dt_bias=_pad_lanes(ssd_dt_bias), a_log=_pad_lanes(ssd_a_log),
        ssd_d=row3(jnp.repeat(ssd_d, SSD_HEAD_DIM, axis=-1)), ssd_norm=row3(ssd_norm),
        fox_bf=_pad_lanes(fox_bf, F_LANE0),
        bblk=bblk, a_re=a_re, a_im=a_im, cblk=cblk, s5_d=row3(s5_d), w_glu=s5_w_glu.astype(BF16),
        w_branch=w_branch.astype(BF16), w_out=w_out.astype(BF16), norm2=row3(norm2),
        w_ffn_in=w_ffn_in.astype(BF16), w_ffn_out=w_ffn_out.astype(BF16),
    )

    def layer(xr, p):
        z, xbc, qkv, u, gate, small = _inproj(xr, p['norm1'], p['w_in'])
        seqv = lambda a: a.reshape(batch, lp, a.shape[1])
        small_s = seqv(small)
        y_a, ak, aq = _ssd(seqv(z), seqv(xbc), small_s, p['conv_w'], p['conv_b'], p['dt_bias'],
                           p['a_log'], p['fox_bf'], p['ssd_d'], p['ssd_norm'], batch=batch)
        y_a = y_a.reshape(rows, D_MODEL)
        y_b = _attention(seqv(qkv), ak, aq, batch=batch).reshape(rows, D_MODEL)
        y_c = _s5(seqv(u), p['bblk'], p['a_re'], p['a_im'], p['cblk'], p['s5_d'],
                  p['w_glu']).reshape(rows, D_MODEL)
        x1 = _merge(y_a, y_b, y_c, gate, xr, p['w_branch'], p['w_out'])
        return _ffn(x1, p['norm2'], p['w_ffn_in'], p['w_ffn_out']), None

    xr, _ = lax.scan(layer, xt, layers)
    out = _final_norm(xr, norm_f.astype(F32)[None, :]).reshape(batch, lp, D_MODEL)
    return out[:, N_META:length]
```

```python
import functools

import jax
import jax.numpy as jnp
from jax import lax
from jax.experimental import pallas as pl
from jax.experimental.pallas import tpu as pltpu

F32 = jnp.float32
BF16 = jnp.bfloat16

D_MODEL = 1024
N_META = 16
CHUNK = 128
EPS = 1e-6
NEG = -1e30

SSD_HEADS = 16
SSD_HEAD_DIM = 64
SSD_GROUPS = 2
SSD_STATE = 128
SSD_CONV = 4
SSD_CONV_DIM = D_MODEL + 2 * SSD_GROUPS * SSD_STATE

FOX_HEADS = 8
FOX_HEAD_DIM = 128

S5_GROUP = 16
S5_GROUPS = D_MODEL // S5_GROUP
S5_STATE = 64
S5_BLOCKS = 8
S5_BLOCK_GROUPS = S5_GROUPS // S5_BLOCKS
S5_HALF = S5_BLOCK_GROUPS * S5_STATE

D_FF = 2816
SMALL_W = 128
F_LANE0 = SSD_HEADS

SEC_WIDTHS = (D_MODEL, SSD_CONV_DIM, 3 * D_MODEL, D_MODEL, 3 * D_MODEL, SMALL_W)
D_IN_P = sum(SEC_WIDTHS)

ROW_TILE = 512
COL_CHUNK = 512
ATT_TILE = 512
ATT_AHEAD = 3
ATT_KEY_TILES = 2
S5_STEPS = 16
VMEM_LIMIT = 56 * 1024 * 1024


def _cparams(sems):
    return pltpu.CompilerParams(dimension_semantics=sems, vmem_limit_bytes=VMEM_LIMIT)


def _resident(shape):
    nd = len(shape)
    return pl.BlockSpec(shape, lambda *_: (0,) * nd, pipeline_mode=pl.Buffered(1))


def _sigmoid(x):
    return 1.0 / (1.0 + jnp.exp(-x))


def _softplus(x):
    return jnp.maximum(x, 0.0) + jnp.log(1.0 + jnp.exp(-jnp.abs(x)))


def _bf16_part(x):
    bits = lax.bitcast_convert_type(x, jnp.uint32) & jnp.uint32(0xFFFF0000)
    return lax.bitcast_convert_type(bits, F32)


def _rms(x, w):
    ms = jnp.mean(x * x, axis=-1, keepdims=True)
    return (x * lax.rsqrt(ms + EPS)) * w


def _inproj_kernel(x_ref, nw_ref, w_ref, *out_refs):
    xn = _rms(x_ref[...], nw_ref[...]).astype(BF16)
    off = 0
    for ref in out_refs:
        width = ref.shape[1]
        for c0 in range(0, width, COL_CHUNK):
            cw = min(COL_CHUNK, width - c0)
            ref[:, c0:c0 + cw] = jnp.dot(
                xn, w_ref[:, off + c0:off + c0 + cw], preferred_element_type=F32
            ).astype(ref.dtype)
        off += width


def _inproj(x, nw, w):
    rows = x.shape[0]
    dtypes = (BF16, BF16, BF16, BF16, BF16, F32)
    return pl.pallas_call(
        _inproj_kernel,
        grid=(rows // ROW_TILE,),
        in_specs=[
            pl.BlockSpec((ROW_TILE, D_MODEL), lambda i: (i, 0)),
            _resident((1, D_MODEL)),
            _resident((D_MODEL, D_IN_P)),
        ],
        out_specs=[pl.BlockSpec((ROW_TILE, wd), lambda i: (i, 0)) for wd in SEC_WIDTHS],
        out_shape=[jax.ShapeDtypeStruct((rows, wd), dt) for wd, dt in zip(SEC_WIDTHS, dtypes)],
        compiler_params=_cparams(("parallel",)),
        name="inproj",
    )(x, nw, w)


def _ssd_consts():
    t = jnp.arange(CHUNK)
    cols = jnp.arange(2 * CHUNK)
    shift = jnp.concatenate(
        [(cols[None, :] == (CHUNK + t - k)[:, None]) for k in range(1, SSD_CONV)], axis=0)
    head_of_lane = jnp.arange(D_MODEL) // SSD_HEAD_DIM
    expand = jnp.arange(SMALL_W)[:, None] == head_of_lane[None, :]
    return shift.astype(BF16), expand.astype(BF16), expand.astype(F32)


def _ssd_kernel(z_ref, xbc_ref, sm_ref, cw_ref, cb_ref, dtb_ref, alog_ref, bf_ref, dsk_ref, nw_ref,
                shift_ref, exp_ref, exp32_ref, sel_ref, ones_ref,
                o_ref, ak_ref, aq_ref, prev_sc, state_sc, fcarry_sc):
    c = pl.program_id(1)

    @pl.when(c == 0)
    def _():
        prev_sc[...] = jnp.zeros_like(prev_sc)
        state_sc[...] = jnp.zeros_like(state_sc)
        fcarry_sc[...] = jnp.zeros_like(fcarry_sc)

    cur_bf = xbc_ref[...]
    both = jnp.concatenate([prev_sc[(c + 1) % 2], cur_bf], axis=0)
    prev_sc[c % 2] = cur_bf
    shifted = jnp.dot(shift_ref[...], both, preferred_element_type=F32)
    conv = cb_ref[...] + cw_ref[SSD_CONV - 1:SSD_CONV, :] * cur_bf.astype(F32)
    for k in range(1, SSD_CONV):
        conv = conv + (cw_ref[SSD_CONV - 1 - k:SSD_CONV - k, :]
                       * shifted[(k - 1) * CHUNK:k * CHUNK])
    xc = conv * _sigmoid(conv)
    xs = xc[:, :D_MODEL]
    n_bc = SSD_GROUPS * SSD_STATE
    bm = xc[:, D_MODEL:D_MODEL + n_bc]
    cm = xc[:, D_MODEL + n_bc:]

    sm = sm_ref[...]
    lane = lax.broadcasted_iota(jnp.int32, (CHUNK, SMALL_W), 1)
    dt = _softplus(sm + dtb_ref[...])
    xf = sm + bf_ref[...]
    logf = jnp.minimum(xf, 0.0) - jnp.log(1.0 + jnp.exp(-jnp.abs(xf)))
    is_ssd = lane < SSD_HEADS
    row = lax.broadcasted_iota(jnp.int32, (CHUNK, CHUNK), 0)
    col = lax.broadcasted_iota(jnp.int32, (CHUNK, CHUNK), 1)
    causal = row >= col
    tri = jnp.where(causal, 1.0, 0.0).astype(F32)
    cs = jnp.dot(tri, jnp.where(is_ssd, dt * (-jnp.exp(alog_ref[...])), logf),
                 preferred_element_type=F32, precision=lax.Precision.HIGHEST)

    fcum = cs + fcarry_sc[...]
    fcarry_sc[...] = jnp.where(is_ssd[:1], 0.0, fcum[CHUNK - 1:CHUNK, :])
    c2 = fcum * LOG2E
    hi = _bf16_part(c2)
    mid = _bf16_part(c2 - hi)
    lo = (c2 - hi) - mid
    pieces = jnp.concatenate([hi, mid, lo], axis=1).astype(BF16)
    aug = jnp.dot(pieces, sel_ref[...], preferred_element_type=F32) + ones_ref[...]
    ak_ref[...] = aug[:, :SMALL_W].astype(BF16)
    aq_ref[...] = aug[:, SMALL_W:].astype(BF16)

    acs = cs
    alast = acs[CHUNK - 1:CHUNK, :]
    acs_t = acs.T
    per_head = jnp.concatenate([dt, jnp.exp(acs), jnp.exp(alast - acs) * dt], axis=0)
    expd = jnp.dot(per_head.astype(BF16), exp_ref[...], preferred_element_type=F32)
    dt_x, eacs_x, w_x = expd[:CHUNK], expd[CHUNK:2 * CHUNK], expd[2 * CHUNK:]
    dec_x = jnp.dot(jnp.broadcast_to(jnp.exp(alast), (8, SMALL_W)), exp32_ref[...],
                    preferred_element_type=F32, precision=lax.Precision.HIGHEST)[:1]
    xdt = (xs * dt_x).astype(BF16)
    xw = (xs * w_x).astype(BF16)
    state = state_sc[...]
    state_bf = state.astype(BF16)

    half = lax.broadcasted_iota(jnp.int32, (CHUNK, 2 * SSD_HEAD_DIM), 1) < SSD_HEAD_DIM
    hpg = SSD_HEADS // SSD_GROUPS
    gw = hpg * SSD_HEAD_DIM
    y_diag, y_off, s_new = [], [], []
    for g in range(SSD_GROUPS):
        bg = bm[:, g * SSD_STATE:(g + 1) * SSD_STATE]
        cg = cm[:, g * SSD_STATE:(g + 1) * SSD_STATE].astype(BF16)
        cb = lax.dot_general(cg, bg.astype(BF16), (((1,), (1,)), ((), ())),
                             preferred_element_type=F32)
        y_off.append(jnp.dot(cg, state_bf[:, g * gw:(g + 1) * gw], preferred_element_type=F32))
        s_new.append(jnp.dot(bg.T.astype(BF16), xw[:, g * gw:(g + 1) * gw],
                             preferred_element_type=F32))
        for pair in range(hpg // 2):
            gm = []
            for h in (g * hpg + 2 * pair, g * hpg + 2 * pair + 1):
                lmat = jnp.where(causal, jnp.exp(acs[:, h:h + 1] - acs_t[h:h + 1, :]), 0.0)
                gm.append((cb * lmat).astype(BF16))
            k0 = (g * hpg + 2 * pair) * SSD_HEAD_DIM
            xp = xdt[:, k0:k0 + 2 * SSD_HEAD_DIM]
            zero = jnp.zeros_like(xp)
            rhs = jnp.concatenate([jnp.where(half, xp, zero), jnp.where(half, zero, xp)], axis=0)
            y_diag.append(jnp.dot(jnp.concatenate(gm, axis=1), rhs, preferred_element_type=F32))

    state_sc[...] = state * dec_x + jnp.concatenate(s_new, axis=1)
    y = (jnp.concatenate(y_diag, axis=1) + jnp.concatenate(y_off, axis=1) * eacs_x
         + xs * dsk_ref[...])
    zf = z_ref[...].astype(F32)
    y = y * (zf * _sigmoid(zf))
    o_ref[...] = _rms(y, nw_ref[...]).astype(o_ref.dtype)


def _ssd(z, xbc, small, cw, cb, dtb, alog, bf, dsk, nw, *, batch):
    lp = z.shape[1]
    seq_blk = lambda w: pl.BlockSpec((None, CHUNK, w), lambda b, c: (b, c, 0))
    consts = _ssd_consts() + _aug_select()
    return pl.pallas_call(
        _ssd_kernel,
        grid=(batch, lp // CHUNK),
        in_specs=[
            seq_blk(D_MODEL), seq_blk(SSD_CONV_DIM), seq_blk(SMALL_W),
            _resident((SSD_CONV, SSD_CONV_DIM)), _resident((1, SSD_CONV_DIM)),
            _resident((1, SMALL_W)), _resident((1, SMALL_W)), _resident((1, SMALL_W)),
            _resident((1, D_MODEL)), _resident((1, D_MODEL)),
        ] + [_resident(a.shape) for a in consts],
        out_specs=[seq_blk(D_MODEL), seq_blk(SMALL_W), seq_blk(SMALL_W)],
        out_shape=[jax.ShapeDtypeStruct((batch, lp, D_MODEL), BF16),
                   jax.ShapeDtypeStruct((batch, lp, SMALL_W), BF16),
                   jax.ShapeDtypeStruct((batch, lp, SMALL_W), BF16)],
        scratch_shapes=[
            pltpu.VMEM((2, CHUNK, SSD_CONV_DIM), BF16),
            pltpu.VMEM((SSD_STATE, D_MODEL), F32),
            pltpu.VMEM((1, SMALL_W), F32),
        ],
        compiler_params=_cparams(("parallel", "arbitrary")),
        name="ssd",
    )(z, xbc, small, cw, cb, dtb, alog, bf, dsk, nw, *consts)


AUG_LANES = 6
LOG2E = 1.4426950408889634


def _aug_select():
    sel = [[0.0] * (2 * SMALL_W) for _ in range(3 * SMALL_W)]
    ones = [0.0] * (2 * SMALL_W)
    for h in range(FOX_HEADS):
        for p in range(3):
            sel[p * SMALL_W + F_LANE0 + h][AUG_LANES * h + p] = -1.0
            sel[p * SMALL_W + F_LANE0 + h][SMALL_W + AUG_LANES * h + 3 + p] = 1.0
            ones[AUG_LANES * h + 3 + p] = 1.0
            ones[SMALL_W + AUG_LANES * h + p] = 1.0
    return jnp.array(sel, BF16), jnp.array([ones], F32)


def _attn_kernel(q_ref, k_ref, v_ref, ak_ref, aq_ref, o_ref, vt_sc, m_sc, l_sc, acc_sc, s_sc,
                 *, lp):
    tq = ATT_TILE
    hd = FOX_HEAD_DIM
    i = pl.program_id(1)
    n_full = lp // tq
    tail = lp - n_full * tq
    heads = range(FOX_HEADS)

    @pl.when(i == 0)
    def _():
        def tr(c, _):
            r = pl.multiple_of(c * CHUNK, CHUNK)
            for h in heads:
                vt_sc[h, :, pl.ds(r, CHUNK)] = (
                    v_ref[pl.ds(r, CHUNK), h * hd:(h + 1) * hd].astype(F32).T.astype(BF16))
            return 0
        lax.fori_loop(0, lp // CHUNK, tr, 0)

    lane = lax.broadcasted_iota(jnp.int32, (tq, SMALL_W), 1)
    aq = aq_ref[...]
    qf = []
    for h in heads:
        mine = (lane >= AUG_LANES * h) & (lane < AUG_LANES * (h + 1))
        qf.append(jnp.concatenate([q_ref[:, h * hd:(h + 1) * hd],
                                   jnp.where(mine, aq, jnp.zeros_like(aq))], axis=1))

    def tile(nq, diag_tk):
        for h in heads:
            m_sc[h, :, :nq] = jnp.full((1, nq), NEG, F32)
            l_sc[h, :, :nq] = jnp.zeros((1, nq), F32)
            acc_sc[h, :, :nq] = jnp.zeros((hd, nq), F32)

        def block(ks, tk, masked):
            akb = ak_ref[pl.ds(ks, tk), :]
            if masked:
                row = lax.broadcasted_iota(jnp.int32, (tk, nq), 0)
                col = lax.broadcasted_iota(jnp.int32, (tk, nq), 1)
                keep = row <= col

            def scores(h):
                kb = jnp.concatenate([k_ref[pl.ds(ks, tk), h * hd:(h + 1) * hd], akb], axis=1)
                s_sc[h % ATT_AHEAD, :tk, :nq] = lax.dot_general(
                    kb, qf[h][:nq], (((1,), (1,)), ((), ())), preferred_element_type=F32)

            for h in range(ATT_AHEAD - 1):
                scores(h)
            for h in heads:
                if h + ATT_AHEAD - 1 < FOX_HEADS:
                    scores(h + ATT_AHEAD - 1)
                s = s_sc[h % ATT_AHEAD, :tk, :nq]
                if masked:
                    s = jnp.where(keep, s, NEG)
                m = m_sc[h, :, :nq]
                m_new = jnp.maximum(m, jnp.max(s, axis=0, keepdims=True))
                p = jnp.exp2(s - m_new)
                alpha = jnp.exp2(m - m_new)
                m_sc[h, :, :nq] = m_new
                l_sc[h, :, :nq] = alpha * l_sc[h, :, :nq] + jnp.sum(p, axis=0, keepdims=True)
                acc_sc[h, :, :nq] = alpha * acc_sc[h, :, :nq] + jnp.dot(
                    vt_sc[h, :, pl.ds(ks, tk)], p.astype(BF16), preferred_element_type=F32)

        big = ATT_KEY_TILES * tq

        def big_block(kj, _):
            block(pl.multiple_of(kj * big, big), big, False)
            return 0

        def full_block(kj, _):
            block(pl.multiple_of(((i // ATT_KEY_TILES) * ATT_KEY_TILES + kj) * tq, tq), tq, False)
            return 0

        lax.fori_loop(0, i // ATT_KEY_TILES, big_block, 0)
        lax.fori_loop(0, i % ATT_KEY_TILES, full_block, 0)
        block(pl.multiple_of(i * tq, tq), diag_tk, True)
        for h in heads:
            o_ref[:nq, h * hd:(h + 1) * hd] = (
                acc_sc[h, :, :nq] / l_sc[h, :, :nq]).T.astype(o_ref.dtype)

    if tail:
        pl.when(i < n_full)(lambda: tile(tq, tq))
        pl.when(i == n_full)(lambda: tile(tail, tail))
    else:
        tile(tq, tq)


def _attention(qkv, ak, aq, *, batch):
    lp = qkv.shape[1]
    hd = FOX_HEAD_DIM
    nh = FOX_HEADS
    width = nh * hd
    once = dict(pipeline_mode=pl.Buffered(1))
    return pl.pallas_call(
        functools.partial(_attn_kernel, lp=lp),
        grid=(batch, pl.cdiv(lp, ATT_TILE)),
        in_specs=[
            pl.BlockSpec((None, ATT_TILE, width), lambda b, i: (b, i, 0)),
            pl.BlockSpec((None, lp, width), lambda b, i: (b, 0, 1), **once),
            pl.BlockSpec((None, lp, width), lambda b, i: (b, 0, 2), **once),
            pl.BlockSpec((None, lp, SMALL_W), lambda b, i: (b, 0, 0), **once),
            pl.BlockSpec((None, ATT_TILE, SMALL_W), lambda b, i: (b, i, 0)),
        ],
        out_specs=pl.BlockSpec((None, ATT_TILE, width), lambda b, i: (b, i, 0)),
        out_shape=jax.ShapeDtypeStruct((batch, lp, width), BF16),
        scratch_shapes=[
            pltpu.VMEM((nh, hd, lp), BF16),
            pltpu.VMEM((nh, 1, ATT_TILE), F32),
            pltpu.VMEM((nh, 1, ATT_TILE), F32),
            pltpu.VMEM((nh, hd, ATT_TILE), F32),
            pltpu.VMEM((ATT_AHEAD, ATT_KEY_TILES * ATT_TILE, ATT_TILE), F32),
        ],
        compiler_params=_cparams(("parallel", "arbitrary")),
        name="fox_attn",
    )(qkv, qkv, qkv, ak, aq)


def _s5_kernel(u_ref, perm_ref, perm_t_ref, bblk_ref, are_ref, aim_ref, cblk_ref, dsk_ref,
               wglu_ref, o_ref, h_sc, carry_sc, y_sc, *, batch):
    blk = 2 * S5_HALF

    @pl.when(pl.program_id(0) == 0)
    def _():
        carry_sc[...] = jnp.zeros_like(carry_sc)

    u_bt = u_ref[...].reshape(batch * S5_STEPS, D_MODEL)
    u = jnp.dot(perm_ref[...], u_bt, preferred_element_type=F32).astype(BF16)
    for j in range(S5_BLOCKS):
        h_sc[:, j * blk:(j + 1) * blk] = jnp.dot(
            u[:, j * 128:(j + 1) * 128], bblk_ref[j], preferred_element_type=F32)

    for j in range(S5_BLOCKS):
        re_l = slice(j * blk, j * blk + S5_HALF)
        im_l = slice(j * blk + S5_HALF, (j + 1) * blk)
        a_re = are_ref[:, j * S5_HALF:(j + 1) * S5_HALF]
        a_im = aim_ref[:, j * S5_HALF:(j + 1) * S5_HALF]
        h_re = carry_sc[:, re_l]
        h_im = carry_sc[:, im_l]
        for t in range(S5_STEPS):
            rows_t = slice(t * batch, (t + 1) * batch)
            n_re = a_re * h_re - a_im * h_im + h_sc[rows_t, re_l]
            n_im = a_re * h_im + a_im * h_re + h_sc[rows_t, im_l]
            h_sc[rows_t, re_l] = n_re
            h_sc[rows_t, im_l] = n_im
            h_re, h_im = n_re, n_im
        carry_sc[:, re_l] = h_re
        carry_sc[:, im_l] = h_im

    for j in range(S5_BLOCKS):
        y_sc[:, j * 128:(j + 1) * 128] = jnp.dot(
            h_sc[:, j * blk:(j + 1) * blk].astype(BF16), cblk_ref[j], preferred_element_type=F32)

    y = y_sc[...] + dsk_ref[...] * u.astype(F32)
    y = 0.5 * y * (1.0 + jnp.tanh(0.7978845608028654 * (y + 0.044715 * (y * y * y))))
    gl = jnp.dot(y.astype(BF16), wglu_ref[...], preferred_element_type=F32)
    out_tb = (y * _sigmoid(gl)).astype(BF16)
    out_bt = jnp.dot(perm_t_ref[...], out_tb, preferred_element_type=F32).astype(o_ref.dtype)
    o_ref[...] = out_bt.reshape(batch, S5_STEPS, D_MODEL)


def _s5(u, bblk, a_re, a_im, cblk, dsk, wglu):
    batch, lp, _ = u.shape
    rows = S5_STEPS * batch
    n_state = S5_BLOCKS * 2 * S5_HALF
    tblk = pl.BlockSpec((batch, S5_STEPS, D_MODEL), lambda i: (0, i, 0))
    src = (jnp.arange(rows) % batch) * S5_STEPS + jnp.arange(rows) // batch
    perm = (src[:, None] == jnp.arange(rows)[None, :]).astype(BF16)
    return pl.pallas_call(
        functools.partial(_s5_kernel, batch=batch),
        grid=(lp // S5_STEPS,),
        in_specs=[
            tblk,
            _resident((rows, rows)), _resident((rows, rows)),
            _resident((S5_BLOCKS, 128, 2 * S5_HALF)),
            _resident((1, S5_BLOCKS * S5_HALF)), _resident((1, S5_BLOCKS * S5_HALF)),
            _resident((S5_BLOCKS, 2 * S5_HALF, 128)),
            _resident((1, D_MODEL)),
            _resident((D_MODEL, D_MODEL)),
        ],
        out_specs=tblk,
        out_shape=jax.ShapeDtypeStruct((batch, lp, D_MODEL), BF16),
        scratch_shapes=[
            pltpu.VMEM((rows, n_state), F32),
            pltpu.VMEM((batch, n_state), F32),
            pltpu.VMEM((rows, D_MODEL), F32),
        ],
        compiler_params=_cparams(("arbitrary",)),
        name="s5",
    )(u, perm, perm.T, bblk, a_re, a_im, cblk, dsk, wglu)


def _merge_kernel(ya_ref, yb_ref, yc_ref, gate_ref, x_ref, wb_ref, wo_ref, o_ref):
    acc = None
    for n, y_ref in enumerate((ya_ref, yb_ref, yc_ref)):
        br = jnp.dot(y_ref[...], wb_ref[n], preferred_element_type=F32)
        gt = _sigmoid(gate_ref[:, n * D_MODEL:(n + 1) * D_MODEL].astype(F32))
        acc = gt * br if acc is None else acc + gt * br
    o_ref[...] = x_ref[...] + jnp.dot(acc.astype(BF16), wo_ref[...], preferred_element_type=F32)


def _merge(ya, yb, yc, gate, x, wb, wo):
    rows = x.shape[0]
    rblk = lambda w: pl.BlockSpec((ROW_TILE, w), lambda i: (i, 0))
    return pl.pallas_call(
        _merge_kernel,
        grid=(rows // ROW_TILE,),
        in_specs=[rblk(D_MODEL), rblk(D_MODEL), rblk(D_MODEL), rblk(3 * D_MODEL), rblk(D_MODEL),
                  _resident((3, D_MODEL, D_MODEL)), _resident((D_MODEL, D_MODEL))],
        out_specs=rblk(D_MODEL),
        out_shape=jax.ShapeDtypeStruct((rows, D_MODEL), F32),
        compiler_params=_cparams(("parallel",)),
        name="merge",
    )(ya, yb, yc, gate, x, wb, wo)


def _ffn_kernel(x_ref, nw_ref, w1_ref, w2_ref, o_ref):
    x = x_ref[...]
    xn = _rms(x, nw_ref[...]).astype(BF16)
    gp = jnp.dot(xn, w1_ref[:, :D_FF], preferred_element_type=F32)
    up = jnp.dot(xn, w1_ref[:, D_FF:], preferred_element_type=F32)
    act = (gp * _sigmoid(gp) * up).astype(BF16)
    o_ref[...] = x + jnp.dot(act, w2_ref[...], preferred_element_type=F32)


def _ffn(x, nw, w1, w2):
    rows = x.shape[0]
    return pl.pallas_call(
        _ffn_kernel,
        grid=(rows // ROW_TILE,),
        in_specs=[pl.BlockSpec((ROW_TILE, D_MODEL), lambda i: (i, 0)), _resident((1, D_MODEL)),
                  _resident((D_MODEL, 2 * D_FF)), _resident((D_FF, D_MODEL))],
        out_specs=pl.BlockSpec((ROW_TILE, D_MODEL), lambda i: (i, 0)),
        out_shape=jax.ShapeDtypeStruct((rows, D_MODEL), F32),
        compiler_params=_cparams(("parallel",)),
        name="ffn",
    )(x, nw, w1, w2)


def _final_norm_kernel(x_ref, nw_ref, o_ref):
    o_ref[...] = _rms(x_ref[...], nw_ref[...])


def _final_norm(x, nw):
    rows = x.shape[0]
    return pl.pallas_call(
        _final_norm_kernel,
        grid=(rows // ROW_TILE,),
        in_specs=[pl.BlockSpec((ROW_TILE, D_MODEL), lambda i: (i, 0)), _resident((1, D_MODEL))],
        out_specs=pl.BlockSpec((ROW_TILE, D_MODEL), lambda i: (i, 0)),
        out_shape=jax.ShapeDtypeStruct((rows, D_MODEL), F32),
        compiler_params=_cparams(("parallel",)),
        name="final_norm",
    )(x, nw)


def _pad_lanes(v, lane0=0):
    depth, n = v.shape
    out = jnp.zeros((depth, 1, SMALL_W), F32)
    return out.at[:, 0, lane0:lane0 + n].set(v.astype(F32))


def _prep_w_in(w_in):
    o = [0, D_MODEL, D_MODEL + SSD_CONV_DIM]
    o.append(o[-1] + SSD_HEADS)
    o.append(o[-1] + 3 * D_MODEL)
    o.append(o[-1] + FOX_HEADS)
    o.append(o[-1] + D_MODEL)
    o.append(o[-1] + 3 * D_MODEL)
    z, xbc, dt, qkv, fr, u, gate = [w_in[..., o[k]:o[k + 1]] for k in range(7)]
    pad = jnp.zeros(w_in.shape[:-1] + (SMALL_W - SSD_HEADS - FOX_HEADS,), w_in.dtype)
    q_scale = jnp.concatenate([jnp.full((D_MODEL,), FOX_HEAD_DIM ** -0.5 * LOG2E, w_in.dtype),
                               jnp.ones((2 * D_MODEL,), w_in.dtype)])
    return jnp.concatenate([z, xbc, qkv * q_scale, u, gate, dt, fr, pad], axis=-1).astype(BF16)


def _prep_s5(lam_re, lam_im, b_re, b_im, c_re, c_im, log_step):
    depth = lam_re.shape[0]
    lam = lax.complex(lam_re.astype(F32), lam_im.astype(F32))
    step = jnp.exp(log_step.astype(F32))[..., None]
    lam_bar = jnp.exp(lam * step)
    b_bar = ((lam_bar - 1.0) / lam)[..., None] * lax.complex(b_re.astype(F32), b_im.astype(F32))
    nb, gb, p, ch = S5_BLOCKS, S5_BLOCK_GROUPS, S5_STATE, S5_GROUP
    eye = jnp.eye(gb, dtype=F32)

    def b_block(part):
        part = part.reshape(depth, nb, gb, p, ch)
        m = jnp.einsum('djgpc,gh->djgchp', part, eye)
        return m.reshape(depth, nb, gb * ch, gb * p)

    bblk = jnp.concatenate([b_block(jnp.real(b_bar)), b_block(jnp.imag(b_bar))], axis=-1)

    def c_block(part):
        part = part.reshape(depth, nb, gb, ch, p)
        m = jnp.einsum('djgcp,gh->djgphc', part, eye)
        return m.reshape(depth, nb, gb * p, gb * ch)

    cblk = jnp.concatenate([c_block(c_re.astype(F32)), c_block(-c_im.astype(F32))], axis=-2)
    a_re = jnp.real(lam_bar).reshape(depth, 1, nb * gb * p)
    a_im = jnp.imag(lam_bar).reshape(depth, 1, nb * gb * p)
    return bblk.astype(BF16), a_re, a_im, cblk.astype(BF16)


def kernel(x, meta, norm1, w_in, ssd_conv_w, ssd_conv_b, ssd_dt_bias, ssd_a_log, ssd_d, ssd_norm,
           fox_bf, s5_lam_re, s5_lam_im, s5_b_re, s5_b_im, s5_c_re, s5_c_im, s5_log_step, s5_d,
           s5_w_glu, w_branch, w_out, norm2, w_ffn_in, w_ffn_out, norm_f):
    batch, seq, _ = x.shape
    length = seq + N_META
    lp = -(-length // CHUNK) * CHUNK
    assert batch % 8 == 0 and (lp * batch) % ROW_TILE == 0 and lp % S5_STEPS == 0
    rows = lp * batch

    xt = jnp.concatenate([
        jnp.broadcast_to(meta[None].astype(F32), (batch, N_META, D_MODEL)),
        x.astype(F32),
        jnp.zeros((batch, lp - length, D_MODEL), F32)], axis=1).reshape(rows, D_MODEL)

    bblk, a_re, a_im, cblk = _prep_s5(s5_lam_re, s5_lam_im, s5_b_re, s5_b_im, s5_c_re, s5_c_im,
                                      s5_log_step)
    row3 = lambda v: v.astype(F32)[:, None, :]
    layers = dict(
        norm1=row3(norm1), w_in=_prep_w_in(w_in),
        conv_w=ssd_conv_w.astype(F32), conv_b=row3(ssd_conv_b),
        dt_bias=_pad_lanes(ssd_dt_bias), a_log=_pad_lanes(ssd_a_log),
        ssd_d=row3(jnp.repeat(ssd_d, SSD_HEAD_DIM, axis=-1)), ssd_norm=row3(ssd_norm),
        fox_bf=_pad_lanes(fox_bf, F_LANE0),
        bblk=bblk, a_re=a_re, a_im=a_im, cblk=cblk, s5_d=row3(s5_d), w_glu=s5_w_glu.astype(BF16),
        w_branch=w_branch.astype(BF16), w_out=w_out.astype(BF16), norm2=row3(norm2),
        w_ffn_in=w_ffn_in.astype(BF16), w_ffn_out=w_ffn_out.astype(BF16),
    )

    def layer(xr, p):
        z, xbc, qkv, u, gate, small = _inproj(xr, p['norm1'], p['w_in'])
        seqv = lambda a: a.reshape(batch, lp, a.shape[1])
        small_s = seqv(small)
        y_a, ak, aq = _ssd(seqv(z), seqv(xbc), small_s, p['conv_w'], p['conv_b'], p['dt_bias'],
                           p['a_log'], p['fox_bf'], p['ssd_d'], p['ssd_norm'], batch=batch)
        y_a = y_a.reshape(rows, D_MODEL)
        y_b = _attention(seqv(qkv), ak, aq, batch=batch).reshape(rows, D_MODEL)
        y_c = _s5(seqv(u), p['bblk'], p['a_re'], p['a_im'], p['cblk'], p['s5_d'],
                  p['w_glu']).reshape(rows, D_MODEL)
        x1 = _merge(y_a, y_b, y_c, gate, xr, p['w_branch'], p['w_out'])
        return _ffn(x1, p['norm2'], p['w_ffn_in'], p['w_ffn_out']), None

    xr, _ = lax.scan(layer, xt, layers)
    out = _final_norm(xr, norm_f.astype(F32)[None, :]).reshape(batch, lp, D_MODEL)
    return out[:, N_META:length]
```

```python
import functools

import jax
import jax.numpy as jnp
from jax import lax
from jax.experimental import pallas as pl
from jax.experimental.pallas import tpu as pltpu

F32 = jnp.float32
BF16 = jnp.bfloat16

D_MODEL = 1024
N_META = 16
CHUNK = 128
EPS = 1e-6
NEG = -1e30

SSD_HEADS = 16
SSD_HEAD_DIM = 64
SSD_GROUPS = 2
SSD_STATE = 128
SSD_CONV = 4
SSD_CONV_DIM = D_MODEL + 2 * SSD_GROUPS * SSD_STATE

FOX_HEADS = 8
FOX_HEAD_DIM = 128

S5_GROUP = 16
S5_GROUPS = D_MODEL // S5_GROUP
S5_STATE = 64
S5_BLOCKS = 8
S5_BLOCK_GROUPS = S5_GROUPS // S5_BLOCKS
S5_HALF = S5_BLOCK_GROUPS * S5_STATE

D_FF = 2816
SMALL_W = 128
F_LANE0 = SSD_HEADS

SEC_WIDTHS = (D_MODEL, SSD_CONV_DIM, 3 * D_MODEL, D_MODEL, 3 * D_MODEL, SMALL_W)
D_IN_P = sum(SEC_WIDTHS)

ROW_TILE = 512
COL_CHUNK = 512
ATT_TILE = 512
ATT_AHEAD = 3
ATT_KEY_TILES = 2
S5_STEPS = 16
VMEM_LIMIT = 56 * 1024 * 1024


def _cparams(sems):
    return pltpu.CompilerParams(dimension_semantics=sems, vmem_limit_bytes=VMEM_LIMIT)


def _resident(shape):
    nd = len(shape)
    return pl.BlockSpec(shape, lambda *_: (0,) * nd, pipeline_mode=pl.Buffered(1))


def _sigmoid(x):
    return 1.0 / (1.0 + jnp.exp(-x))


def _softplus(x):
    return jnp.maximum(x, 0.0) + jnp.log(1.0 + jnp.exp(-jnp.abs(x)))


def _bf16_part(x):
    bits = lax.bitcast_convert_type(x, jnp.uint32) & jnp.uint32(0xFFFF0000)
    return lax.bitcast_convert_type(bits, F32)


def _rms(x, w):
    ms = jnp.mean(x * x, axis=-1, keepdims=True)
    return (x * lax.rsqrt(ms + EPS)) * w


def _inproj_kernel(x_ref, nw_ref, w_ref, *out_refs):
    xn = _rms(x_ref[...], nw_ref[...]).astype(BF16)
    off = 0
    for ref in out_refs:
        width = ref.shape[1]
        for c0 in range(0, width, COL_CHUNK):
            cw = min(COL_CHUNK, width - c0)
            ref[:, c0:c0 + cw] = jnp.dot(
                xn, w_ref[:, off + c0:off + c0 + cw], preferred_element_type=F32
            ).astype(ref.dtype)
        off += width


def _inproj(x, nw, w):
    rows = x.shape[0]
    dtypes = (BF16, BF16, BF16, BF16, BF16, F32)
    return pl.pallas_call(
        _inproj_kernel,
        grid=(rows // ROW_TILE,),
        in_specs=[
            pl.BlockSpec((ROW_TILE, D_MODEL), lambda i: (i, 0)),
            _resident((1, D_MODEL)),
            _resident((D_MODEL, D_IN_P)),
        ],
        out_specs=[pl.BlockSpec((ROW_TILE, wd), lambda i: (i, 0)) for wd in SEC_WIDTHS],
        out_shape=[jax.ShapeDtypeStruct((rows, wd), dt) for wd, dt in zip(SEC_WIDTHS, dtypes)],
        compiler_params=_cparams(("parallel",)),
        name="inproj",
    )(x, nw, w)


def _ssd_consts():
    t = jnp.arange(CHUNK)
    cols = jnp.arange(2 * CHUNK)
    shift = jnp.concatenate(
        [(cols[None, :] == (CHUNK + t - k)[:, None]) for k in range(1, SSD_CONV)], axis=0)
    head_of_lane = jnp.arange(D_MODEL) // SSD_HEAD_DIM
    expand = jnp.arange(SMALL_W)[:, None] == head_of_lane[None, :]
    return shift.astype(BF16), expand.astype(BF16), expand.astype(F32)


def _ssd_kernel(z_ref, xbc_ref, sm_ref, cw_ref, cb_ref, dtb_ref, alog_ref, bf_ref, dsk_ref, nw_ref,
                shift_ref, exp_ref, exp32_ref, sel_ref, ones_ref,
                o_ref, ak_ref, aq_ref, prev_sc, state_sc, fcarry_sc):
    c = pl.program_id(1)

    @pl.when(c == 0)
    def _():
        prev_sc[...] = jnp.zeros_like(prev_sc)
        state_sc[...] = jnp.zeros_like(state_sc)
        fcarry_sc[...] = jnp.zeros_like(fcarry_sc)

    cur_bf = xbc_ref[...]
    both = jnp.concatenate([prev_sc[(c + 1) % 2], cur_bf], axis=0)
    prev_sc[c % 2] = cur_bf
    shifted = jnp.dot(shift_ref[...], both, preferred_element_type=F32)
    conv = cb_ref[...] + cw_ref[SSD_CONV - 1:SSD_CONV, :] * cur_bf.astype(F32)
    for k in range(1, SSD_CONV):
        conv = conv + (cw_ref[SSD_CONV - 1 - k:SSD_CONV - k, :]
                       * shifted[(k - 1) * CHUNK:k * CHUNK])
    xc = conv * _sigmoid(conv)
    xs = xc[:, :D_MODEL]
    n_bc = SSD_GROUPS * SSD_STATE
    bm = xc[:, D_MODEL:D_MODEL + n_bc]
    cm = xc[:, D_MODEL + n_bc:]

    sm = sm_ref[...]
    lane = lax.broadcasted_iota(jnp.int32, (CHUNK, SMALL_W), 1)
    dt = _softplus(sm + dtb_ref[...])
    xf = sm + bf_ref[...]
    logf = jnp.minimum(xf, 0.0) - jnp.log(1.0 + jnp.exp(-jnp.abs(xf)))
    is_ssd = lane < SSD_HEADS
    row = lax.broadcasted_iota(jnp.int32, (CHUNK, CHUNK), 0)
    col = lax.broadcasted_iota(jnp.int32, (CHUNK, CHUNK), 1)
    causal = row >= col
    tri = jnp.where(causal, 1.0, 0.0).astype(F32)
    cs = jnp.dot(tri, jnp.where(is_ssd, dt * (-jnp.exp(alog_ref[...])), logf),
                 preferred_element_type=F32, precision=lax.Precision.HIGHEST)

    fcum = cs + fcarry_sc[...]
    fcarry_sc[...] = jnp.where(is_ssd[:1], 0.0, fcum[CHUNK - 1:CHUNK, :])
    c2 = fcum * LOG2E
    hi = _bf16_part(c2)
    mid = _bf16_part(c2 - hi)
    lo = (c2 - hi) - mid
    pieces = jnp.concatenate([hi, mid, lo], axis=1).astype(BF16)
    aug = jnp.dot(pieces, sel_ref[...], preferred_element_type=F32) + ones_ref[...]
    ak_ref[...] = aug[:, :SMALL_W].astype(BF16)
    aq_ref[...] = aug[:, SMALL_W:].astype(BF16)

    acs = cs
    alast = acs[CHUNK - 1:CHUNK, :]
    acs_t = acs.T
    per_head = jnp.concatenate([dt, jnp.exp(acs), jnp.exp(alast - acs) * dt], axis=0)
    expd = jnp.dot(per_head.astype(BF16), exp_ref[...], preferred_element_type=F32)
    dt_x, eacs_x, w_x = expd[:CHUNK], expd[CHUNK:2 * CHUNK], expd[2 * CHUNK:]
    dec_x = jnp.dot(jnp.broadcast_to(jnp.exp(alast), (8, SMALL_W)), exp32_ref[...],
                    preferred_element_type=F32, precision=lax.Precision.HIGHEST)[:1]
    xdt = (xs * dt_x).astype(BF16)
    xw = (xs * w_x).astype(BF16)
    state = state_sc[...]
    state_bf = state.astype(BF16)

    half = lax.broadcasted_iota(jnp.int32, (CHUNK, 2 * SSD_HEAD_DIM), 1) < SSD_HEAD_DIM
    hpg = SSD_HEADS // SSD_GROUPS
    gw = hpg * SSD_HEAD_DIM
    y_diag, y_off, s_new = [], [], []
    for g in range(SSD_GROUPS):
        bg = bm[:, g * SSD_STATE:(g + 1) * SSD_STATE]
        cg = cm[:, g * SSD_STATE:(g + 1) * SSD_STATE].astype(BF16)
        cb = lax.dot_general(cg, bg.astype(BF16), (((1,), (1,)), ((), ())),
                             preferred_element_type=F32)
        y_off.append(jnp.dot(cg, state_bf[:, g * gw:(g + 1) * gw], preferred_element_type=F32))
        s_new.append(jnp.dot(bg.T.astype(BF16), xw[:, g * gw:(g + 1) * gw],
                             preferred_element_type=F32))
        for pair in range(hpg // 2):
            gm = []
            for h in (g * hpg + 2 * pair, g * hpg + 2 * pair + 1):
                lmat = jnp.where(causal, jnp.exp(acs[:, h:h + 1] - acs_t[h:h + 1, :]), 0.0)
                gm.append((cb * lmat).astype(BF16))
            k0 = (g * hpg + 2 * pair) * SSD_HEAD_DIM
            xp = xdt[:, k0:k0 + 2 * SSD_HEAD_DIM]
            zero = jnp.zeros_like(xp)
            rhs = jnp.concatenate([jnp.where(half, xp, zero), jnp.where(half, zero, xp)], axis=0)
            y_diag.append(jnp.dot(jnp.concatenate(gm, axis=1), rhs, preferred_element_type=F32))

    state_sc[...] = state * dec_x + jnp.concatenate(s_new, axis=1)
    y = (jnp.concatenate(y_diag, axis=1) + jnp.concatenate(y_off, axis=1) * eacs_x
         + xs * dsk_ref[...])
    zf = z_ref[...].astype(F32)
    y = y * (zf * _sigmoid(zf))
    o_ref[...] = _rms(y, nw_ref[...]).astype(o_ref.dtype)


def _ssd(z, xbc, small, cw, cb, dtb, alog, bf, dsk, nw, *, batch):
    lp = z.shape[1]
    seq_blk = lambda w: pl.BlockSpec((None, CHUNK, w), lambda b, c: (b, c, 0))
    consts = _ssd_consts() + _aug_select()
    return pl.pallas_call(
        _ssd_kernel,
        grid=(batch, lp // CHUNK),
        in_specs=[
            seq_blk(D_MODEL), seq_blk(SSD_CONV_DIM), seq_blk(SMALL_W),
            _resident((SSD_CONV, SSD_CONV_DIM)), _resident((1, SSD_CONV_DIM)),
            _resident((1, SMALL_W)), _resident((1, SMALL_W)), _resident((1, SMALL_W)),
            _resident((1, D_MODEL)), _resident((1, D_MODEL)),
        ] + [_resident(a.shape) for a in consts],
        out_specs=[seq_blk(D_MODEL), seq_blk(SMALL_W), seq_blk(SMALL_W)],
        out_shape=[jax.ShapeDtypeStruct((batch, lp, D_MODEL), BF16),
                   jax.ShapeDtypeStruct((batch, lp, SMALL_W), BF16),
                   jax.ShapeDtypeStruct((batch, lp, SMALL_W), BF16)],
        scratch_shapes=[
            pltpu.VMEM((2, CHUNK, SSD_CONV_DIM), BF16),
            pltpu.VMEM((SSD_STATE, D_MODEL), F32),
            pltpu.VMEM((1, SMALL_W), F32),
        ],
        compiler_params=_cparams(("parallel", "arbitrary")),
        name="ssd",
    )(z, xbc, small, cw, cb, dtb, alog, bf, dsk, nw, *consts)


AUG_LANES = 6
LOG2E = 1.4426950408889634


def _aug_select():
    sel = [[0.0] * (2 * SMALL_W) for _ in range(3 * SMALL_W)]
    ones = [0.0] * (2 * SMALL_W)
    for h in range(FOX_HEADS):
        for p in range(3):
            sel[p * SMALL_W + F_LANE0 + h][AUG_LANES * h + p] = -1.0
            sel[p * SMALL_W + F_LANE0 + h][SMALL_W + AUG_LANES * h + 3 + p] = 1.0
            ones[AUG_LANES * h + 3 + p] = 1.0
            ones[SMALL_W + AUG_LANES * h + p] = 1.0
    return jnp.array(sel, BF16), jnp.array([ones], F32)


def _attn_kernel(q_ref, k_ref, v_ref, ak_ref, aq_ref, o_ref, vt_sc, m_sc, l_sc, acc_sc, s_sc,
                 *, lp):
    tq = ATT_TILE
    hd = FOX_HEAD_DIM
    i = pl.program_id(1)
    n_full = lp // tq
    tail = lp - n_full * tq
    heads = range(FOX_HEADS)

    @pl.when(i == 0)
    def _():
        def tr(c, _):
            r = pl.multiple_of(c * CHUNK, CHUNK)
            for h in heads:
                vt_sc[h, :, pl.ds(r, CHUNK)] = (
                    v_ref[pl.ds(r, CHUNK), h * hd:(h + 1) * hd].astype(F32).T.astype(BF16))
            return 0
        lax.fori_loop(0, lp // CHUNK, tr, 0)

    lane = lax.broadcasted_iota(jnp.int32, (tq, SMALL_W), 1)
    aq = aq_ref[...]
    qf = []
    for h in heads:
        mine = (lane >= AUG_LANES * h) & (lane < AUG_LANES * (h + 1))
        qf.append(jnp.concatenate([q_ref[:, h * hd:(h + 1) * hd],
                                   jnp.where(mine, aq, jnp.zeros_like(aq))], axis=1))

    def tile(nq, diag_tk):
        for h in heads:
            m_sc[h, :, :nq] = jnp.full((1, nq), NEG, F32)
            l_sc[h, :, :nq] = jnp.zeros((1, nq), F32)
            acc_sc[h, :, :nq] = jnp.zeros((hd, nq), F32)

        def block(ks, tk, masked):
            akb = ak_ref[pl.ds(ks, tk), :]
            if masked:
                row = lax.broadcasted_iota(jnp.int32, (tk, nq), 0)
                col = lax.broadcasted_iota(jnp.int32, (tk, nq), 1)
                keep = row <= col

            def scores(h):
                kb = jnp.concatenate([k_ref[pl.ds(ks, tk), h * hd:(h + 1) * hd], akb], axis=1)
                s_sc[h % ATT_AHEAD, :tk, :nq] = lax.dot_general(
                    kb, qf[h][:nq], (((1,), (1,)), ((), ())), preferred_element_type=F32)

            for h in range(ATT_AHEAD - 1):
                scores(h)
            for h in heads:
                if h + ATT_AHEAD - 1 < FOX_HEADS:
                    scores(h + ATT_AHEAD - 1)
                s = s_sc[h % ATT_AHEAD, :tk, :nq]
                if masked:
                    s = jnp.where(keep, s, NEG)
                m = m_sc[h, :, :nq]
                m_new = jnp.maximum(m, jnp.max(s, axis=0, keepdims=True))
                p = jnp.exp2(s - m_new)
                alpha = jnp.exp2(m - m_new)
                m_sc[h, :, :nq] = m_new
                l_sc[h, :, :nq] = alpha * l_sc[h, :, :nq] + jnp.sum(p, axis=0, keepdims=True)
                acc_sc[h, :, :nq] = alpha * acc_sc[h, :, :nq] + jnp.dot(
                    vt_sc[h, :, pl.ds(ks, tk)], p.astype(BF16), preferred_element_type=F32)

        big = ATT_KEY_TILES * tq

        def big_block(kj, _):
            block(pl.multiple_of(kj * big, big), big, False)
            return 0

        def full_block(kj, _):
            block(pl.multiple_of(((i // ATT_KEY_TILES) * ATT_KEY_TILES + kj) * tq, tq), tq, False)
            return 0

        lax.fori_loop(0, i // ATT_KEY_TILES, big_block, 0)
        lax.fori_loop(0, i % ATT_KEY_TILES, full_block, 0)
        block(pl.multiple_of(i * tq, tq), diag_tk, True)
        for h in heads:
            o_ref[:nq, h * hd:(h + 1) * hd] = (
                acc_sc[h, :, :nq] / l_sc[h, :, :nq]).T.astype(o_ref.dtype)

    if tail:
        pl.when(i < n_full)(lambda: tile(tq, tq))
        pl.when(i == n_full)(lambda: tile(tail, tail))
    else:
        tile(tq, tq)


def _attention(qkv, ak, aq, *, batch):
    lp = qkv.shape[1]
    hd = FOX_HEAD_DIM
    nh = FOX_HEADS
    width = nh * hd
    once = dict(pipeline_mode=pl.Buffered(1))
    return pl.pallas_call(
        functools.partial(_attn_kernel, lp=lp),
        grid=(batch, pl.cdiv(lp, ATT_TILE)),
        in_specs=[
            pl.BlockSpec((None, ATT_TILE, width), lambda b, i: (b, i, 0)),
            pl.BlockSpec((None, lp, width), lambda b, i: (b, 0, 1), **once),
            pl.BlockSpec((None, lp, width), lambda b, i: (b, 0, 2), **once),
            pl.BlockSpec((None, lp, SMALL_W), lambda b, i: (b, 0, 0), **once),
            pl.BlockSpec((None, ATT_TILE, SMALL_W), lambda b, i: (b, i, 0)),
        ],
        out_specs=pl.BlockSpec((None, ATT_TILE, width), lambda b, i: (b, i, 0)),
        out_shape=jax.ShapeDtypeStruct((batch, lp, width), BF16),
        scratch_shapes=[
            pltpu.VMEM((nh, hd, lp), BF16),
            pltpu.VMEM((nh, 1, ATT_TILE), F32),
            pltpu.VMEM((nh, 1, ATT_TILE), F32),
            pltpu.VMEM((nh, hd, ATT_TILE), F32),
            pltpu.VMEM((ATT_AHEAD, ATT_KEY_TILES * ATT_TILE, ATT_TILE), F32),
        ],
        compiler_params=_cparams(("parallel", "arbitrary")),
        name="fox_attn",
    )(qkv, qkv, qkv, ak, aq)


def _s5_kernel(u_ref, perm_ref, perm_t_ref, bblk_ref, are_ref, aim_ref, cblk_ref, dsk_ref,
               wglu_ref, o_ref, h_sc, carry_sc, y_sc, *, batch):
    blk = 2 * S5_HALF

    @pl.when(pl.program_id(0) == 0)
    def _():
        carry_sc[...] = jnp.zeros_like(carry_sc)

    u_bt = u_ref[...].reshape(batch * S5_STEPS, D_MODEL)
    u = jnp.dot(perm_ref[...], u_bt, preferred_element_type=F32).astype(BF16)
    for j in range(S5_BLOCKS):
        h_sc[:, j * blk:(j + 1) * blk] = jnp.dot(
            u[:, j * 128:(j + 1) * 128], bblk_ref[j], preferred_element_type=F32)

    for j in range(S5_BLOCKS):
        re_l = slice(j * blk, j * blk + S5_HALF)
        im_l = slice(j * blk + S5_HALF, (j + 1) * blk)
        a_re = are_ref[:, j * S5_HALF:(j + 1) * S5_HALF]
        a_im = aim_ref[:, j * S5_HALF:(j + 1) * S5_HALF]
        h_re = carry_sc[:, re_l]
        h_im = carry_sc[:, im_l]
        for t in range(S5_STEPS):
            rows_t = slice(t * batch, (t + 1) * batch)
            n_re = a_re * h_re - a_im * h_im + h_sc[rows_t, re_l]
            n_im = a_re * h_im + a_im * h_re + h_sc[rows_t, im_l]
            h_sc[rows_t, re_l] = n_re
            h_sc[rows_t, im_l] = n_im
            h_re, h_im = n_re, n_im
        carry_sc[:, re_l] = h_re
        carry_sc[:, im_l] = h_im

    for j in range(S5_BLOCKS):
        y_sc[:, j * 128:(j + 1) * 128] = jnp.dot(
            h_sc[:, j * blk:(j + 1) * blk].astype(BF16), cblk_ref[j], preferred_element_type=F32)

    y = y_sc[...] + dsk_ref[...] * u.astype(F32)
    y = 0.5 * y * (1.0 + jnp.tanh(0.7978845608028654 * (y + 0.044715 * (y * y * y))))
    gl = jnp.dot(y.astype(BF16), wglu_ref[...], preferred_element_type=F32)
    out_tb = (y * _sigmoid(gl)).astype(BF16)
    out_bt = jnp.dot(perm_t_ref[...], out_tb, preferred_element_type=F32).astype(o_ref.dtype)
    o_ref[...] = out_bt.reshape(batch, S5_STEPS, D_MODEL)


def _s5(u, bblk, a_re, a_im, cblk, dsk, wglu):
    batch, lp, _ = u.shape
    rows = S5_STEPS * batch
    n_state = S5_BLOCKS * 2 * S5_HALF
    tblk = pl.BlockSpec((batch, S5_STEPS, D_MODEL), lambda i: (0, i, 0))
    src = (jnp.arange(rows) % batch) * S5_STEPS + jnp.arange(rows) // batch
    perm = (src[:, None] == jnp.arange(rows)[None, :]).astype(BF16)
    return pl.pallas_call(
        functools.partial(_s5_kernel, batch=batch),
        grid=(lp // S5_STEPS,),
        in_specs=[
            tblk,
            _resident((rows, rows)), _resident((rows, rows)),
            _resident((S5_BLOCKS, 128, 2 * S5_HALF)),
            _resident((1, S5_BLOCKS * S5_HALF)), _resident((1, S5_BLOCKS * S5_HALF)),
            _resident((S5_BLOCKS, 2 * S5_HALF, 128)),
            _resident((1, D_MODEL)),
            _resident((D_MODEL, D_MODEL)),
        ],
        out_specs=tblk,
        out_shape=jax.ShapeDtypeStruct((batch, lp, D_MODEL), BF16),
        scratch_shapes=[
            pltpu.VMEM((rows, n_state), F32),
            pltpu.VMEM((batch, n_state), F32),
            pltpu.VMEM((rows, D_MODEL), F32),
        ],
        compiler_params=_cparams(("arbitrary",)),
        name="s5",
    )(u, perm, perm.T, bblk, a_re, a_im, cblk, dsk, wglu)


def _merge_kernel(ya_ref, yb_ref, yc_ref, gate_ref, x_ref, wb_ref, wo_ref, o_ref):
    acc = None
    for n, y_ref in enumerate((ya_ref, yb_ref, yc_ref)):
        br = jnp.dot(y_ref[...], wb_ref[n], preferred_element_type=F32)
        gt = _sigmoid(gate_ref[:, n * D_MODEL:(n + 1) * D_MODEL].astype(F32))
        acc = gt * br if acc is None else acc + gt * br
    o_ref[...] = x_ref[...] + jnp.dot(acc.astype(BF16), wo_ref[...], preferred_element_type=F32)


def _merge(ya, yb, yc, gate, x, wb, wo):
    rows = x.shape[0]
    rblk = lambda w: pl.BlockSpec((ROW_TILE, w), lambda i: (i, 0))
    return pl.pallas_call(
        _merge_kernel,
        grid=(rows // ROW_TILE,),
        in_specs=[rblk(D_MODEL), rblk(D_MODEL), rblk(D_MODEL), rblk(3 * D_MODEL), rblk(D_MODEL),
                  _resident((3, D_MODEL, D_MODEL)), _resident((D_MODEL, D_MODEL))],
        out_specs=rblk(D_MODEL),
        out_shape=jax.ShapeDtypeStruct((rows, D_MODEL), F32),
        compiler_params=_cparams(("parallel",)),
        name="merge",
    )(ya, yb, yc, gate, x, wb, wo)


def _ffn_kernel(x_ref, nw_ref, w1_ref, w2_ref, o_ref):
    x = x_ref[...]
    xn = _rms(x, nw_ref[...]).astype(BF16)
    gp = jnp.dot(xn, w1_ref[:, :D_FF], preferred_element_type=F32)
    up = jnp.dot(xn, w1_ref[:, D_FF:], preferred_element_type=F32)
    act = (gp * _sigmoid(gp) * up).astype(BF16)
    o_ref[...] = x + jnp.dot(act, w2_ref[...], preferred_element_type=F32)


def _ffn(x, nw, w1, w2):
    rows = x.shape[0]
    return pl.pallas_call(
        _ffn_kernel,
        grid=(rows // ROW_TILE,),
        in_specs=[pl.BlockSpec((ROW_TILE, D_MODEL), lambda i: (i, 0)), _resident((1, D_MODEL)),
                  _resident((D_MODEL, 2 * D_FF)), _resident((D_FF, D_MODEL))],
        out_specs=pl.BlockSpec((ROW_TILE, D_MODEL), lambda i: (i, 0)),
        out_shape=jax.ShapeDtypeStruct((rows, D_MODEL), F32),
        compiler_params=_cparams(("parallel",)),
        name="ffn",
    )(x, nw, w1, w2)


def _final_norm_kernel(xa_ref, xb_ref, nw_ref, o_ref):
    x = jnp.concatenate([xa_ref[N_META:, :], xb_ref[:N_META, :]], axis=0)
    o_ref[...] = _rms(x, nw_ref[...])


def _final_norm(x, nw, seq):
    batch = x.shape[0]
    assert seq % ROW_TILE == 0
    return pl.pallas_call(
        _final_norm_kernel,
        grid=(batch, seq // ROW_TILE),
        in_specs=[pl.BlockSpec((None, ROW_TILE, D_MODEL), lambda b, i: (b, i, 0)),
                  pl.BlockSpec((None, ROW_TILE, D_MODEL), lambda b, i: (b, i + 1, 0)),
                  _resident((1, D_MODEL))],
        out_specs=pl.BlockSpec((None, ROW_TILE, D_MODEL), lambda b, i: (b, i, 0)),
        out_shape=jax.ShapeDtypeStruct((batch, seq, D_MODEL), F32),
        compiler_params=_cparams(("parallel", "parallel")),
        name="final_norm",
    )(x, x, nw)


def _pad_lanes(v, lane0=0):
    depth, n = v.shape
    out = jnp.zeros((depth, 1, SMALL_W), F32)
    return out.at[:, 0, lane0:lane0 + n].set(v.astype(F32))


def _prep_w_in(w_in):
    o = [0, D_MODEL, D_MODEL + SSD_CONV_DIM]
    o.append(o[-1] + SSD_HEADS)
    o.append(o[-1] + 3 * D_MODEL)
    o.append(o[-1] + FOX_HEADS)
    o.append(o[-1] + D_MODEL)
    o.append(o[-1] + 3 * D_MODEL)
    z, xbc, dt, qkv, fr, u, gate = [w_in[..., o[k]:o[k + 1]] for k in range(7)]
    pad = jnp.zeros(w_in.shape[:-1] + (SMALL_W - SSD_HEADS - FOX_HEADS,), w_in.dtype)
    q_scale = jnp.concatenate([jnp.full((D_MODEL,), FOX_HEAD_DIM ** -0.5 * LOG2E, w_in.dtype),
                               jnp.ones((2 * D_MODEL,), w_in.dtype)])
    return jnp.concatenate([z, xbc, qkv * q_scale, u, gate, dt, fr, pad], axis=-1).astype(BF16)


def _prep_s5(lam_re, lam_im, b_re, b_im, c_re, c_im, log_step):
    depth = lam_re.shape[0]
    lam = lax.complex(lam_re.astype(F32), lam_im.astype(F32))
    step = jnp.exp(log_step.astype(F32))[..., None]
    lam_bar = jnp.exp(lam * step)
    b_bar = ((lam_bar - 1.0) / lam)[..., None] * lax.complex(b_re.astype(F32), b_im.astype(F32))
    nb, gb, p, ch = S5_BLOCKS, S5_BLOCK_GROUPS, S5_STATE, S5_GROUP
    eye = jnp.eye(gb, dtype=F32)

    def b_block(part):
        part = part.reshape(depth, nb, gb, p, ch)
        m = jnp.einsum('djgpc,gh->djgchp', part, eye)
        return m.reshape(depth, nb, gb * ch, gb * p)

    bblk = jnp.concatenate([b_block(jnp.real(b_bar)), b_block(jnp.imag(b_bar))], axis=-1)

    def c_block(part):
        part = part.reshape(depth, nb, gb, ch, p)
        m = jnp.einsum('djgcp,gh->djgphc', part, eye)
        return m.reshape(depth, nb, gb * p, gb * ch)

    cblk = jnp.concatenate([c_block(c_re.astype(F32)), c_block(-c_im.astype(F32))], axis=-2)
    a_re = jnp.real(lam_bar).reshape(depth, 1, nb * gb * p)
    a_im = jnp.imag(lam_bar).reshape(depth, 1, nb * gb * p)
    return bblk.astype(BF16), a_re, a_im, cblk.astype(BF16)


def kernel(x, meta, norm1, w_in, ssd_conv_w, ssd_conv_b, ssd_dt_bias, ssd_a_log, ssd_d, ssd_norm,
           fox_bf, s5_lam_re, s5_lam_im, s5_b_re, s5_b_im, s5_c_re, s5_c_im, s5_log_step, s5_d,
           s5_w_glu, w_branch, w_out, norm2, w_ffn_in, w_ffn_out, norm_f):
    batch, seq, _ = x.shape
    length = seq + N_META
    lp = -(-length // CHUNK) * CHUNK
    assert batch % 8 == 0 and (lp * batch) % ROW_TILE == 0 and lp % S5_STEPS == 0
    rows = lp * batch

    xt = jnp.concatenate([
        jnp.broadcast_to(meta[None].astype(F32), (batch, N_META, D_MODEL)),
        x.astype(F32),
        jnp.zeros((batch, lp - length, D_MODEL), F32)], axis=1).reshape(rows, D_MODEL)

    bblk, a_re, a_im, cblk = _prep_s5(s5_lam_re, s5_lam_im, s5_b_re, s5_b_im, s5_c_re, s5_c_im,
                                      s5_log_step)
    row3 = lambda v: v.astype(F32)[:, None, :]
    layers = dict(
        norm1=row3(norm1), w_in=_prep_w_in(w_in),
        conv_w=ssd_conv_w.astype(F32), conv_b=row3(ssd_conv_b),
        dt_bias=_pad_lanes(ssd_dt_bias), a_log=_pad_lanes(ssd_a_log),
        ssd_d=row3(jnp.repeat(ssd_d, SSD_HEAD_DIM, axis=-1)), ssd_norm=row3(ssd_norm),
        fox_bf=_pad_lanes(fox_bf, F_LANE0),
        bblk=bblk, a_re=a_re, a_im=a_im, cblk=cblk, s5_d=row3(s5_d), w_glu=s5_w_glu.astype(BF16),
        w_branch=w_branch.astype(BF16), w_out=w_out.astype(BF16), norm2=row3(norm2),
        w_ffn_in=w_ffn_in.astype(BF16), w_ffn_out=w_ffn_out.astype(BF16),
    )

    def layer(xr, p):
        z, xbc, qkv, u, gate, small = _inproj(xr, p['norm1'], p['w_in'])
        seqv = lambda a: a.reshape(batch, lp, a.shape[1])
        small_s = seqv(small)
        y_a, ak, aq = _ssd(seqv(z), seqv(xbc), small_s, p['conv_w'], p['conv_b'], p['dt_bias'],
                           p['a_log'], p['fox_bf'], p['ssd_d'], p['ssd_norm'], batch=batch)
        y_a = y_a.reshape(rows, D_MODEL)
        y_b = _attention(seqv(qkv), ak, aq, batch=batch).reshape(rows, D_MODEL)
        y_c = _s5(seqv(u), p['bblk'], p['a_re'], p['a_im'], p['cblk'], p['s5_d'],
                  p['w_glu']).reshape(rows, D_MODEL)
        x1 = _merge(y_a, y_b, y_c, gate, xr, p['w_branch'], p['w_out'])
        return _ffn(x1, p['norm2'], p['w_ffn_in'], p['w_ffn_out']), None

    xr, _ = lax.scan(layer, xt, layers)
    return _final_norm(xr.reshape(batch, lp, D_MODEL), norm_f.astype(F32)[None, :], seq)
```
